```python
import jax, jax.numpy as jnp
from jax import lax
import numpy as np

D_MODEL = 1024
BATCH = 4
SEQ = 8192
DEPTH = 1
DEC_BATCH = 8
DEC_SEQ = 2048
PAST_LEN = 128

N_ATTN_HEADS = 8
HEAD_DIM = 64
ATTN_WIDTH = N_ATTN_HEADS * HEAD_DIM
CONV_WIDTH = D_MODEL // 2
MIX_WIDTH = ATTN_WIDTH + CONV_WIDTH
IN_PROJ_WIDTH = 3 * ATTN_WIDTH + 2 * CONV_WIDTH
WINDOWS = (128, 512, 2048)
DILATIONS = (1, 4, 16)
ATTN_BLOCK = 128
ROPE_DIM = HEAD_DIM // 4
ROPE_THETA = 500000.0
NEG_INF = -1e30
CONV_KERNEL = 31
N_EXPERTS = 32
TOP_K = 4
D_FF = D_MODEL
SWIGLU_ALPHA = 1.702
SWIGLU_LIMIT = 7.0
MOE_BLOCK = 128
EPS = 1e-6

kernel_name = 'hymba_dilated_conformer_moe_encoder'


def rmsnorm(x, g):
    xf = x.astype(jnp.float32)
    y = xf * lax.rsqrt(jnp.mean(xf * xf, axis=-1, keepdims=True) + EPS)
    return (y * g.astype(jnp.float32)).astype(x.dtype)


def partial_rope(x):
    S = x.shape[1]
    half = ROPE_DIM // 2
    inv_freq = ROPE_THETA ** (-jnp.arange(half, dtype=jnp.float32) * 2.0 / ROPE_DIM)
    ang = jnp.arange(S, dtype=jnp.float32)[:, None] * inv_freq[None, :]
    cos = jnp.cos(ang)[None, :, None, :].astype(x.dtype)
    sin = jnp.sin(ang)[None, :, None, :].astype(x.dtype)
    x1 = x[..., :half]
    x2 = x[..., half:ROPE_DIM]
    rest = x[..., ROPE_DIM:]
    return jnp.concatenate([x1 * cos - x2 * sin, x2 * cos + x1 * sin, rest], axis=-1)


def band_attention(q, k, v, half):
    N, L, H, Dh = q.shape
    blk = ATTN_BLOCK
    nb = -(-L // blk)
    Lp = nb * blk
    qb = jnp.pad(q, ((0, 0), (0, Lp - L), (0, 0), (0, 0))).reshape(N, nb, blk, H, Dh)
    kpad = ((0, 0), (half, Lp + blk - half - L), (0, 0), (0, 0))
    kp = jnp.pad(k, kpad).reshape(N, nb + 1, blk, H, Dh)
    vp = jnp.pad(v, kpad).reshape(N, nb + 1, blk, H, Dh)
    kb = jnp.concatenate([kp[:, :-1], kp[:, 1:]], axis=2)
    vb = jnp.concatenate([vp[:, :-1], vp[:, 1:]], axis=2)
    t = jnp.arange(blk)[:, None]
    u = jnp.arange(2 * blk)[None, :]
    rel_ok = (u >= t) & (u - t <= 2 * half)
    kpos = jnp.arange(nb)[:, None] * blk + jnp.arange(2 * blk)[None, :] - half
    pos_ok = (kpos >= 0) & (kpos < L)
    mask = (rel_ok[None, :, :] & pos_ok[:, None, :])[None, :, None]
    s = jnp.einsum('nbqhd,nbkhd->nbhqk', qb.astype(jnp.float32), kb.astype(jnp.float32)) * (Dh ** -0.5)
    s = jnp.where(mask, s, NEG_INF)
    m = jnp.max(s, axis=-1, keepdims=True)
    p = jnp.where(mask, jnp.exp(s - m), 0.0)
    den = jnp.sum(p, axis=-1)
    o = jnp.einsum('nbhqk,nbkhd->nbqhd', p, vb.astype(jnp.float32))
    o = o.reshape(N, Lp, H, Dh)[:, :L]
    m = m[..., 0].transpose(0, 1, 3, 2).reshape(N, Lp, H)[:, :L]
    den = den.transpose(0, 1, 3, 2).reshape(N, Lp, H)[:, :L]
    return o, m, den


def dilated_attention(q, k, v):
    B, S, H, Dh = q.shape
    outs, maxes, dens = [], [], []
    for w, d in zip(WINDOWS, DILATIONS):
        half = w // (2 * d)
        L = S // d

        def gather_stride(t):
            return t.reshape(B, L, d, H, Dh).transpose(0, 2, 1, 3, 4).reshape(B * d, L, H, Dh)

        o, m, den = band_attention(gather_stride(q), gather_stride(k), gather_stride(v), half)
        outs.append(o.reshape(B, d, L, H, Dh).transpose(0, 2, 1, 3, 4).reshape(B, S, H, Dh))
        maxes.append(m.reshape(B, d, L, H).transpose(0, 2, 1, 3).reshape(B, S, H))
        dens.append(den.reshape(B, d, L, H).transpose(0, 2, 1, 3).reshape(B, S, H))
    m_all = jnp.stack(maxes)
    wts = jnp.exp(m_all - jnp.max(m_all, axis=0, keepdims=True))
    num = sum(wts[i][..., None] * outs[i] for i in range(len(outs)))
    den = sum(wts[i] * dens[i] for i in range(len(dens)))
    out = num / den[..., None]
    return out.reshape(B, S, H * Dh).astype(q.dtype)


def conformer_conv(u, conv_w, conv_b, ln_g, ln_b):
    a = u[..., :CONV_WIDTH]
    g = u[..., CONV_WIDTH:]
    z = a * jax.nn.sigmoid(g)
    pad = CONV_KERNEL // 2
    z = lax.conv_general_dilated(
        z, conv_w.reshape(CONV_KERNEL, 1, CONV_WIDTH).astype(z.dtype),
        window_strides=(1,), padding=((pad, pad),),
        dimension_numbers=('NWC', 'WIO', 'NWC'),
        feature_group_count=CONV_WIDTH) + conv_b
    zf = z.astype(jnp.float32)
    mu = jnp.mean(zf, axis=-1, keepdims=True)
    var = jnp.mean(jnp.square(zf - mu), axis=-1, keepdims=True)
    zf = (zf - mu) * lax.rsqrt(var + EPS) * ln_g.astype(jnp.float32) + ln_b.astype(jnp.float32)
    return jax.nn.silu(zf).astype(u.dtype)


def token_mixer(h, w_in, conv_w, conv_b, conv_ln_g, conv_ln_b, w_out):
    B, S, _ = h.shape
    proj = h @ w_in
    q = proj[..., :ATTN_WIDTH].reshape(B, S, N_ATTN_HEADS, HEAD_DIM)
    k = proj[..., ATTN_WIDTH:2 * ATTN_WIDTH].reshape(B, S, N_ATTN_HEADS, HEAD_DIM)
    v = proj[..., 2 * ATTN_WIDTH:3 * ATTN_WIDTH].reshape(B, S, N_ATTN_HEADS, HEAD_DIM)
    conv_in = proj[..., 3 * ATTN_WIDTH:]
    attn = dilated_attention(partial_rope(q), partial_rope(k), v)
    conv = conformer_conv(conv_in, conv_w, conv_b, conv_ln_g, conv_ln_b)
    return jnp.concatenate([attn, conv], axis=-1) @ w_out


def moe(h, w_router, b_router, w_up, b_up, w_down, b_down):
    Bn, S, D = h.shape
    T = Bn * S
    A = T * TOP_K
    xt = h.reshape(T, D)
    logits = (xt @ w_router + b_router).astype(jnp.float32)
    top_val, top_idx = lax.top_k(logits, TOP_K)
    gates = jax.nn.softmax(top_val, axis=-1).astype(h.dtype)
    e_flat = top_idx.reshape(A)
    tok_flat = jnp.arange(A, dtype=jnp.int32) // TOP_K
    g_flat = gates.reshape(A)
    order = jnp.argsort(e_flat, stable=True)
    e_s = e_flat[order]
    tok_s = tok_flat[order]
    g_s = g_flat[order]
    counts = jnp.bincount(e_flat, length=N_EXPERTS)
    starts = jnp.cumsum(counts) - counts
    pcounts = (counts + MOE_BLOCK - 1) // MOE_BLOCK * MOE_BLOCK
    pends = jnp.cumsum(pcounts)
    pstarts = pends - pcounts
    dest = pstarts[e_s] + (jnp.arange(A, dtype=jnp.int32) - starts[e_s])
    n_pad = A + N_EXPERTS * MOE_BLOCK
    n_blocks = n_pad // MOE_BLOCK
    row_tok = jnp.full((n_pad,), T, dtype=jnp.int32).at[dest].set(tok_s)
    x_rows = jnp.concatenate([xt, jnp.zeros((1, D), xt.dtype)], axis=0)[row_tok]
    x_rows = x_rows.reshape(n_blocks, MOE_BLOCK, D)
    blk_start = jnp.arange(n_blocks, dtype=jnp.int32) * MOE_BLOCK
    blk_exp = jnp.minimum(jnp.sum(pends[None, :] <= blk_start[:, None], axis=1), N_EXPERTS - 1)

    def expert_block(args):
        xb, e = args
        gu = xb @ w_up[e] + b_up[e]
        glu = jnp.minimum(gu[:, :D_FF], SWIGLU_LIMIT)
        lin = jnp.clip(gu[:, D_FF:], -SWIGLU_LIMIT, SWIGLU_LIMIT)
        act = glu * jax.nn.sigmoid(SWIGLU_ALPHA * glu) * (lin + 1.0)
        return act @ w_down[e] + b_down[e]

    y_rows = lax.map(expert_block, (x_rows, blk_exp)).reshape(n_pad, D)
    y = jax.ops.segment_sum(y_rows[dest] * g_s[:, None], tok_s, num_segments=T)
    return y.reshape(Bn, S, D).astype(h.dtype)


def encoder_layer(x, c, w_ada, b_ada, norm_mix_g, w_in, conv_w, conv_b, conv_ln_g, conv_ln_b,
                  w_out, norm_ffn_g, w_router, b_router, w_up, b_up, w_down, b_down):
    mod = (jax.nn.silu(c) @ w_ada + b_ada)[:, None, :]
    shift1, scale1, gate1, shift2, scale2, gate2 = jnp.split(mod, 6, axis=-1)
    h = rmsnorm(x, norm_mix_g) * (1.0 + scale1) + shift1
    x = x + gate1 * token_mixer(h, w_in, conv_w, conv_b, conv_ln_g, conv_ln_b, w_out)
    h = rmsnorm(x, norm_ffn_g) * (1.0 + scale2) + shift2
    x = x + gate2 * moe(h, w_router, b_router, w_up, b_up, w_down, b_down)
    return x


def trunk(x, c, w_ada, b_ada, norm_mix_g, w_in, conv_w, conv_b, conv_ln_g, conv_ln_b,
          w_out, norm_ffn_g, w_router, b_router, w_up, b_up, w_down, b_down, final_g):
    for l in range(DEPTH):
        x = encoder_layer(x, c, w_ada[l], b_ada[l], norm_mix_g[l], w_in[l], conv_w[l], conv_b[l],
                          conv_ln_g[l], conv_ln_b[l], w_out[l], norm_ffn_g[l], w_router[l],
                          b_router[l], w_up[l], b_up[l], w_down[l], b_down[l])
    return rmsnorm(x, final_g)


def setup_inputs(seed: int = 0) -> dict:
    key = jax.random.key(seed)
    ks = jax.random.split(key, 24)
    f32 = jnp.float32
    D, E, F = D_MODEL, N_EXPERTS, D_FF

    def nrm(k, shape, scale):
        return jax.random.normal(k, shape, f32) * scale

    return {
        'x_prompt': nrm(ks[0], (BATCH, SEQ, D), 1.0),
        'x_sample': nrm(ks[1], (DEC_BATCH, DEC_SEQ, D), 1.0),
        'c_prompt': nrm(ks[2], (BATCH, D), 1.0),
        'c_sample': nrm(ks[3], (DEC_BATCH, D), 1.0),
        'w_ada': nrm(ks[4], (DEPTH, D, 6 * D), D ** -0.5),
        'b_ada': nrm(ks[5], (DEPTH, 6 * D), 0.02),
        'norm_mix_g': 1.0 + nrm(ks[6], (DEPTH, D), 0.02),
        'w_in': nrm(ks[7], (DEPTH, D, IN_PROJ_WIDTH), D ** -0.5),
        'conv_w': nrm(ks[8], (DEPTH, CONV_KERNEL, CONV_WIDTH), CONV_KERNEL ** -0.5),
        'conv_b': nrm(ks[9], (DEPTH, CONV_WIDTH), 0.02),
        'conv_ln_g': 1.0 + nrm(ks[10], (DEPTH, CONV_WIDTH), 0.02),
        'conv_ln_b': nrm(ks[11], (DEPTH, CONV_WIDTH), 0.02),
        'w_out': nrm(ks[12], (DEPTH, MIX_WIDTH, D), MIX_WIDTH ** -0.5),
        'norm_ffn_g': 1.0 + nrm(ks[13], (DEPTH, D), 0.02),
        'w_router': nrm(ks[14], (DEPTH, D, E), D ** -0.5),
        'b_router': nrm(ks[15], (DEPTH, E), 0.01),
        'w_up': nrm(ks[16], (DEPTH, E, D, 2 * F), D ** -0.5),
        'b_up': nrm(ks[17], (DEPTH, E, 2 * F), 0.02),
        'w_down': nrm(ks[18], (DEPTH, E, F, D), F ** -0.5),
        'b_down': nrm(ks[19], (DEPTH, E, D), 0.02),
        'final_g': 1.0 + nrm(ks[20], (D,), 0.02),
    }


def reference(x_prompt, x_sample, c_prompt, c_sample, w_ada, b_ada, norm_mix_g, w_in, conv_w,
              conv_b, conv_ln_g, conv_ln_b, w_out, norm_ffn_g, w_router, b_router, w_up, b_up,
              w_down, b_down, final_g):
    y_prompt = trunk(x_prompt, c_prompt, w_ada, b_ada, norm_mix_g, w_in, conv_w, conv_b,
                     conv_ln_g, conv_ln_b, w_out, norm_ffn_g, w_router, b_router, w_up, b_up,
                     w_down, b_down, final_g)
    y_sample = trunk(x_sample, c_sample, w_ada, b_ada, norm_mix_g, w_in, conv_w, conv_b,
                     conv_ln_g, conv_ln_b, w_out, norm_ffn_g, w_router, b_router, w_up, b_up,
                     w_down, b_down, final_g)
    return (y_prompt, y_sample)
```

```python
import functools

import jax
import jax.numpy as jnp
from jax import lax
from jax.experimental import pallas as pl
from jax.experimental.pallas import tpu as pltpu

D_MODEL = 1024
N_HEADS = 8
HEAD_DIM = 64
ATTN_W = N_HEADS * HEAD_DIM
CONV_W = D_MODEL // 2
IN_PROJ_W = 3 * ATTN_W + 2 * CONV_W
WINDOWS = (128, 512, 2048)
DILATIONS = (1, 4, 16)
ATTN_BLOCK = 128
HALF = 64
ROPE_DIM = HEAD_DIM // 4
ROPE_THETA = 500000.0
NEG_INF = -1e30
CONV_K = 31
CONV_PAD = CONV_K // 2
N_EXPERTS = 32
TOP_K = 4
D_FF = D_MODEL
SWIGLU_ALPHA = 1.702
SWIGLU_LIMIT = 7.0
EPS = 1e-6

LANES = 128
VMEM_LIMIT = 56 * 1024 * 1024

F32 = jnp.float32
BF16 = jnp.bfloat16


def _cparams(*sem):
    return pltpu.CompilerParams(dimension_semantics=sem, vmem_limit_bytes=VMEM_LIMIT)


def _mod_kernel(c_ref, w_ref, b_ref, o_ref):
    c = c_ref[...]
    a = (c * jax.nn.sigmoid(c)).astype(BF16)
    o_ref[...] = jnp.dot(a, w_ref[...], preferred_element_type=F32) + b_ref[...]


def _modulation(c_all, w_ada, b_ada):
    rows = c_all.shape[0]
    n = w_ada.shape[1]
    tn = 1536
    return pl.pallas_call(
        _mod_kernel,
        out_shape=jax.ShapeDtypeStruct((rows, n), F32),
        grid=(n // tn,),
        in_specs=[pl.BlockSpec((rows, D_MODEL), lambda j: (0, 0)),
                  pl.BlockSpec((D_MODEL, tn), lambda j: (0, j)),
                  pl.BlockSpec((1, tn), lambda j: (0, j))],
        out_specs=pl.BlockSpec((rows, tn), lambda j: (0, j)),
        compiler_params=_cparams("arbitrary"),
        name="modulation",
    )(c_all, w_ada, b_ada)


def _rmsnorm_rows(x, g):
    return x * lax.rsqrt(jnp.mean(x * x, axis=-1, keepdims=True) + EPS) * g


def _inproj_kernel(x_ref, mod_ref, g_ref, w_ref, cos_ref, s1_ref, s2_ref,
                   q_ref, k_ref, v_ref, z_ref):
    x = x_ref[...]
    shift = mod_ref[0:1, :]
    scale = mod_ref[1:2, :]
    h = _rmsnorm_rows(x, g_ref[...]) * (1.0 + scale) + shift
    hb = h.astype(BF16)
    cos = cos_ref[...]
    s1 = s1_ref[...]
    s2 = s2_ref[...]

    def rope(p):
        return p * cos + pltpu.roll(p, LANES - ROPE_DIM // 2, 1) * s1 + pltpu.roll(p, ROPE_DIM // 2, 1) * s2

    for c in range(ATTN_W // LANES):
        lo = c * LANES
        pq = jnp.dot(hb, w_ref[:, lo:lo + LANES], preferred_element_type=F32)
        q_ref[:, lo:lo + LANES] = (rope(pq) * (HEAD_DIM ** -0.5)).astype(BF16)
        pk = jnp.dot(hb, w_ref[:, ATTN_W + lo:ATTN_W + lo + LANES], preferred_element_type=F32)
        k_ref[:, lo:lo + LANES] = rope(pk).astype(BF16)
    v_ref[...] = jnp.dot(hb, w_ref[:, 2 * ATTN_W:3 * ATTN_W], preferred_element_type=F32).astype(BF16)
    a = jnp.dot(hb, w_ref[:, 3 * ATTN_W:3 * ATTN_W + CONV_W], preferred_element_type=F32)
    g = jnp.dot(hb, w_ref[:, 3 * ATTN_W + CONV_W:], preferred_element_type=F32)
    z_ref[...] = (a * jax.nn.sigmoid(g)).astype(BF16)


def _rope_tables(seq):
    half = ROPE_DIM // 2
    inv_freq = ROPE_THETA ** (-jnp.arange(half, dtype=F32) * 2.0 / ROPE_DIM)
    ang = jnp.arange(seq, dtype=F32)[:, None] * inv_freq[None, :]
    cos = jnp.cos(ang)
    sin = jnp.sin(ang)
    ones = jnp.ones((seq, HEAD_DIM - ROPE_DIM), F32)
    zeros = jnp.zeros((seq, HEAD_DIM - ROPE_DIM), F32)
    zh = jnp.zeros((seq, half), F32)
    cos_h = jnp.concatenate([cos, cos, ones], axis=1)
    s1_h = jnp.concatenate([-sin, zh, zeros], axis=1)
    s2_h = jnp.concatenate([zh, sin, zeros], axis=1)
    rep = LANES // HEAD_DIM
    return tuple(jnp.tile(t, (1, rep)) for t in (cos_h, s1_h, s2_h))


def _in_proj(x2, mod8, mod_off, g1, w_in, tables, batch, seq, tm):
    tokens = batch * seq
    tiles_per_row = seq // tm
    row_spec = lambda w: pl.BlockSpec((tm, w), lambda i: (i, 0))
    tab_spec = pl.BlockSpec((tm, LANES), lambda i: (i % tiles_per_row, 0))
    out = jax.ShapeDtypeStruct((tokens, ATTN_W), BF16)
    return pl.pallas_call(
        _inproj_kernel,
        out_shape=(out, out, out, jax.ShapeDtypeStruct((tokens, CONV_W), BF16)),
        grid=(tokens // tm,),
        in_specs=[row_spec(D_MODEL),
                  pl.BlockSpec((None, 8, D_MODEL), lambda i: (mod_off + i // tiles_per_row, 0, 0)),
                  pl.BlockSpec((1, D_MODEL), lambda i: (0, 0)),
                  pl.BlockSpec((D_MODEL, IN_PROJ_W), lambda i: (0, 0)),
                  tab_spec, tab_spec, tab_spec],
        out_specs=(row_spec(ATTN_W), row_spec(ATTN_W), row_spec(ATTN_W), row_spec(CONV_W)),
        compiler_params=_cparams("arbitrary"),
        name="in_proj",
    )(x2, mod8, g1, w_in, *tables)


def _attn_kernel(q_ref, kp_ref, kc_ref, kn_ref, vp_ref, vc_ref, vn_ref, o_ref, lse_ref,
                 kw_ref, vw_ref, *, tq, length):
    j = pl.program_id(2)
    kw_ref[0:HALF, :] = kp_ref[...]
    kw_ref[HALF:HALF + tq, :] = kc_ref[...]
    kw_ref[HALF + tq:, :] = kn_ref[...]
    vw_ref[0:HALF, :] = vp_ref[...]
    vw_ref[HALF:HALF + tq, :] = vc_ref[...]
    vw_ref[HALF + tq:, :] = vn_ref[...]

    nk = 2 * ATTN_BLOCK
    t_io = lax.broadcasted_iota(jnp.int32, (ATTN_BLOCK, nk), 0)
    u_io = lax.broadcasted_iota(jnp.int32, (ATTN_BLOCK, nk), 1)
    lane = lax.broadcasted_iota(jnp.int32, (ATTN_BLOCK, LANES), 1)
    first_head = lane < HEAD_DIM
    lane1 = lax.broadcasted_iota(jnp.int32, (1, LANES), 1)
    head_keep = [jnp.where(lane1 < HEAD_DIM, 1.0, 0.0).astype(BF16),
                 jnp.where(lane1 < HEAD_DIM, 0.0, 1.0).astype(BF16)]

    for blk in range(tq // ATTN_BLOCK):
        r0 = blk * ATTN_BLOCK
        base = j * tq + r0
        u_min = jnp.maximum(t_io, HALF - base)
        u_max = jnp.minimum(t_io + 2 * HALF, length + HALF - 1 - base)
        mask = (u_io >= u_min) & (u_io <= u_max)
        for c in range(ATTN_W // LANES):
            lo = c * LANES
            qc = q_ref[r0:r0 + ATTN_BLOCK, lo:lo + LANES]
            kc = kw_ref[r0:r0 + nk, lo:lo + LANES]
            vc = vw_ref[r0:r0 + nk, lo:lo + LANES]
            outs = []
            lses = []
            for hh in range(LANES // HEAD_DIM):
                qm = qc * head_keep[hh]
                s = lax.dot_general(qm, kc, (((1,), (1,)), ((), ())), preferred_element_type=F32)
                s = jnp.where(mask, s, NEG_INF)
                m = jnp.max(s, axis=-1, keepdims=True)
                p = jnp.exp(s - m)
                den = jnp.sum(p, axis=-1, keepdims=True)
                pv = jnp.dot(p.astype(BF16), vc, preferred_element_type=F32)
                outs.append(pv / den)
                lses.append(m + jnp.log(den))
            o_ref[r0:r0 + ATTN_BLOCK, lo:lo + LANES] = jnp.where(first_head, outs[0], outs[1]).astype(BF16)
            lse_ref[r0:r0 + ATTN_BLOCK, lo:lo + LANES] = jnp.where(
                first_head, jnp.broadcast_to(lses[0], (ATTN_BLOCK, LANES)),
                jnp.broadcast_to(lses[1], (ATTN_BLOCK, LANES)))


def _attention_pattern(q, k, v, batch, seq, d):
    length = seq // d
    tq = min(length, 512)
    view = lambda t: t.reshape(batch, length, d * ATTN_W)
    nh = tq // HALF
    last_h = length // HALF - 1
    cur = pl.BlockSpec((None, tq, ATTN_W), lambda b, r, j: (b, j, r))
    prev = pl.BlockSpec((None, HALF, ATTN_W), lambda b, r, j: (b, jnp.maximum(j * nh - 1, 0), r))
    nxt = pl.BlockSpec((None, HALF, ATTN_W), lambda b, r, j: (b, jnp.minimum((j + 1) * nh, last_h), r))
    o, lse = pl.pallas_call(
        functools.partial(_attn_kernel, tq=tq, length=length),
        out_shape=(jax.ShapeDtypeStruct((batch, length, d * ATTN_W), BF16),
                   jax.ShapeDtypeStruct((batch, length, d * ATTN_W), F32)),
        grid=(batch, d, length // tq),
        in_specs=[cur, prev, cur, nxt, prev, cur, nxt],
        out_specs=(cur, cur),
        scratch_shapes=[pltpu.VMEM((tq + 2 * HALF, ATTN_W), BF16),
                        pltpu.VMEM((tq + 2 * HALF, ATTN_W), BF16)],
        compiler_params=_cparams("arbitrary", "arbitrary", "arbitrary"),
        name=f"attention_d{d}",
    )(view(q), view(k), view(k), view(k), view(v), view(v), view(v))
    tokens = batch * seq
    return o.reshape(tokens, ATTN_W), lse.reshape(tokens, ATTN_W)


POST_TM = 256
CONV_HALO = 16
CONV_ROWS = 64
ROUTER_W = 128


def _post_kernel(x_ref, mod_ref, o1_ref, o2_ref, o3_ref, l1_ref, l2_ref, l3_ref,
                 zp_ref, zc_ref, zn_ref, cw_ref, cb_ref, lng_ref, lnb_ref,
                 wo_ref, g2_ref, wr_ref, br_ref, cnt0_ref,
                 xn_ref, h2_ref, e_ref, r_ref, gt_ref, cnt_ref,
                 zw_ref, conv_ref, carry_ref, *, tiles_per_row):
    tm = POST_TM
    i = pl.program_id(0)

    @pl.when(i == 0)
    def _():
        carry_ref[...] = cnt0_ref[...]

    pos = i % tiles_per_row
    zp = zp_ref[...].astype(F32)
    zn = zn_ref[...].astype(F32)
    zw_ref[0:CONV_HALO, :] = jnp.where(pos == 0, 0.0, zp)
    zw_ref[CONV_HALO:CONV_HALO + tm, :] = zc_ref[...].astype(F32)
    zw_ref[CONV_HALO + tm:, :] = jnp.where(pos == tiles_per_row - 1, 0.0, zn)
    cw = cw_ref[...]
    off = CONV_HALO - CONV_PAD
    for rc in range(tm // CONV_ROWS):
        r0 = rc * CONV_ROWS
        acc = jnp.broadcast_to(cb_ref[...], (CONV_ROWS, CONV_W))
        for tap in range(CONV_K):
            acc = acc + zw_ref[r0 + off + tap:r0 + off + tap + CONV_ROWS, :] * cw[tap:tap + 1, :]
        mu = jnp.mean(acc, axis=-1, keepdims=True)
        cen = acc - mu
        var = jnp.mean(cen * cen, axis=-1, keepdims=True)
        zf = cen * lax.rsqrt(var + EPS) * lng_ref[...] + lnb_ref[...]
        conv_ref[r0:r0 + CONV_ROWS, :] = (zf * jax.nn.sigmoid(zf)).astype(BF16)

    la, lb, lc = l1_ref[...], l2_ref[...], l3_ref[...]
    lmax = jnp.maximum(jnp.maximum(la, lb), lc)
    wa, wb, wc = jnp.exp(la - lmax), jnp.exp(lb - lmax), jnp.exp(lc - lmax)
    num = wa * o1_ref[...].astype(F32) + wb * o2_ref[...].astype(F32) + wc * o3_ref[...].astype(F32)
    attn = (num / (wa + wb + wc)).astype(BF16)

    mix = (jnp.dot(attn, wo_ref[0:ATTN_W, :], preferred_element_type=F32)
           + jnp.dot(conv_ref[...], wo_ref[ATTN_W:, :], preferred_element_type=F32))
    gate1 = mod_ref[2:3, :]
    xn = x_ref[...] + gate1 * mix
    xn_ref[...] = xn
    h2 = _rmsnorm_rows(xn, g2_ref[...]) * (1.0 + mod_ref[4:5, :]) + mod_ref[3:4, :]
    h2_ref[...] = h2

    hi = h2.astype(BF16)
    lo = (h2 - hi.astype(F32)).astype(BF16)
    wr = wr_ref[...]
    both = jnp.dot(hi, wr, preferred_element_type=F32) + jnp.dot(lo, wr, preferred_element_type=F32)
    bt = both.T
    logit = bt[0:N_EXPERTS, :] + bt[N_EXPERTS:2 * N_EXPERTS, :] + br_ref[...]

    row = lax.broadcasted_iota(jnp.int32, (N_EXPERTS, tm), 0).astype(F32)
    work = logit
    sel = jnp.zeros((N_EXPERTS, tm), F32)
    idxs, vals, hots = [], [], []
    for _ in range(TOP_K):
        m = jnp.max(work, axis=0, keepdims=True)
        idx = jnp.min(jnp.where(work == m, row, float(N_EXPERTS)), axis=0, keepdims=True)
        hot = row == idx
        work = jnp.where(hot, -jnp.inf, work)
        sel = jnp.where(hot, 1.0, sel)
        idxs.append(idx.astype(jnp.int32))
        vals.append(m)
        hots.append(hot)
    exps = [jnp.exp(v - vals[0]) for v in vals]
    esum = exps[0] + exps[1] + exps[2] + exps[3]
    gates = [e / esum for e in exps]

    tr = lax.broadcasted_iota(jnp.int32, (tm, tm), 0)
    tc = lax.broadcasted_iota(jnp.int32, (tm, tm), 1)
    before = jnp.where(tr < tc, 1.0, 0.0).astype(BF16)
    cnt = jnp.dot(sel.astype(BF16), before, preferred_element_type=F32) + carry_ref[:, 0:1]
    ranks = [jnp.sum(jnp.where(h, cnt, 0.0), axis=0, keepdims=True).astype(jnp.int32) for h in hots]
    carry_ref[...] = carry_ref[...] + jnp.sum(sel, axis=1, keepdims=True)
    cnt_ref[...] = carry_ref[...]

    row8 = lax.broadcasted_iota(jnp.int32, (8, tm), 0)
    e8 = jnp.zeros((8, tm), jnp.int32)
    r8 = jnp.zeros((8, tm), jnp.int32)
    for k in range(TOP_K):
        e8 = jnp.where(row8 == k, idxs[k], e8)
        r8 = jnp.where(row8 == k, ranks[k], r8)
    e_ref[...] = e8
    r_ref[...] = r8
    rowg = lax.broadcasted_iota(jnp.int32, (LANES, tm), 0)
    g_t = jnp.zeros((LANES, tm), F32)
    for k in range(TOP_K):
        g_t = jnp.where(rowg == k, gates[k], g_t)
    gt_ref[...] = g_t.T


def _post(x2, mod8, mod_off, attn_outs, z, conv_w, conv_b, ln_g, ln_b, w_out, g2, w_router2, b_router_t,
          cnt0, batch, seq):
    tm = POST_TM
    tokens = batch * seq
    tiles_per_row = seq // tm
    nh = tm // CONV_HALO
    last_h = tokens // CONV_HALO - 1
    row = lambda w: pl.BlockSpec((tm, w), lambda i: (i, 0))
    full = lambda a, b: pl.BlockSpec((a, b), lambda i: (0, 0))
    (o1, l1), (o2, l2), (o3, l3) = attn_outs
    out_shapes = (jax.ShapeDtypeStruct((tokens, D_MODEL), F32),
                  jax.ShapeDtypeStruct((tokens, D_MODEL), F32),
                  jax.ShapeDtypeStruct((8, tokens), jnp.int32),
                  jax.ShapeDtypeStruct((8, tokens), jnp.int32),
                  jax.ShapeDtypeStruct((tokens, LANES), F32),
                  jax.ShapeDtypeStruct((N_EXPERTS, LANES), F32))
    return pl.pallas_call(
        functools.partial(_post_kernel, tiles_per_row=tiles_per_row),
        out_shape=out_shapes,
        grid=(tokens // tm,),
        in_specs=[row(D_MODEL),
                  pl.BlockSpec((None, 8, D_MODEL), lambda i: (mod_off + i // tiles_per_row, 0, 0)),
                  row(ATTN_W), row(ATTN_W), row(ATTN_W), row(ATTN_W), row(ATTN_W), row(ATTN_W),
                  pl.BlockSpec((CONV_HALO, CONV_W), lambda i: (jnp.maximum(i * nh - 1, 0), 0)),
                  row(CONV_W),
                  pl.BlockSpec((CONV_HALO, CONV_W), lambda i: (jnp.minimum((i + 1) * nh, last_h), 0)),
                  full(32, CONV_W), full(1, CONV_W), full(1, CONV_W), full(1, CONV_W),
                  full(D_MODEL, D_MODEL), full(1, D_MODEL), full(D_MODEL, ROUTER_W),
                  full(N_EXPERTS, tm), full(N_EXPERTS, LANES)],
        out_specs=(row(D_MODEL), row(D_MODEL),
                   pl.BlockSpec((8, tm), lambda i: (0, i)), pl.BlockSpec((8, tm), lambda i: (0, i)),
                   row(LANES), full(N_EXPERTS, LANES)),
        scratch_shapes=[pltpu.VMEM((tm + 2 * CONV_HALO, CONV_W), F32),
                        pltpu.VMEM((tm, CONV_W), BF16),
                        pltpu.VMEM((N_EXPERTS, LANES), F32)],
        compiler_params=_cparams("arbitrary"),
        name="post",
    )(x2, mod8, o1, o2, o3, l1, l2, l3, z, z, z, conv_w, conv_b, ln_g, ln_b, w_out, g2,
      w_router2, b_router_t, cnt0)


DISPATCH_TM = 256
MOE_BM = 256


def _dispatch_kernel(pstart_ref, e_ref, r_ref, h_ref, zero_ref, rows_ref, sem):
    del zero_ref
    tm = DISPATCH_TM

    def body(t, carry):
        for k in range(TOP_K):
            dest = pstart_ref[e_ref[k, t]] + r_ref[k, t]
            pltpu.make_async_copy(h_ref.at[pl.ds(t, 1)], rows_ref.at[pl.ds(dest, 1)], sem).start()
        return carry

    lax.fori_loop(0, tm, body, 0)
    for _ in range(TOP_K):
        pltpu.make_async_copy(h_ref, rows_ref.at[pl.ds(0, tm)], sem).wait()


def _dispatch(pstart, e8, r8, h2, n_pad):
    tm = DISPATCH_TM
    tokens = h2.shape[0]
    smem = pl.BlockSpec((8, tm), lambda i, ps: (0, i), memory_space=pltpu.SMEM)
    return pl.pallas_call(
        _dispatch_kernel,
        out_shape=jax.ShapeDtypeStruct((n_pad, D_MODEL), F32),
        grid_spec=pltpu.PrefetchScalarGridSpec(
            num_scalar_prefetch=1, grid=(tokens // tm,),
            in_specs=[smem, smem, pl.BlockSpec((tm, D_MODEL), lambda i, ps: (i, 0)),
                      pl.BlockSpec(memory_space=pl.ANY)],
            out_specs=pl.BlockSpec(memory_space=pl.ANY),
            scratch_shapes=[pltpu.SemaphoreType.DMA(())]),
        input_output_aliases={4: 0},
        compiler_params=_cparams("arbitrary"),
        name="dispatch",
    )(pstart, e8, r8, h2, jnp.zeros((n_pad, D_MODEL), F32))


FF_CHUNK = 512


def _experts_kernel(bexp_ref, used_ref, x_ref, wu_ref, bu_ref, wd_ref, bd_ref, y_ref):
    del bexp_ref

    @pl.when(pl.program_id(0) < used_ref[0])
    def _():
        xb = x_ref[...].astype(BF16)
        acc = jnp.broadcast_to(bd_ref[...], (MOE_BM, D_MODEL))
        for c in range(D_FF // FF_CHUNK):
            lo = c * FF_CHUNK
            glu = jnp.dot(xb, wu_ref[:, lo:lo + FF_CHUNK], preferred_element_type=F32) + bu_ref[:, lo:lo + FF_CHUNK]
            lin = (jnp.dot(xb, wu_ref[:, D_FF + lo:D_FF + lo + FF_CHUNK], preferred_element_type=F32)
                   + bu_ref[:, D_FF + lo:D_FF + lo + FF_CHUNK])
            glu = jnp.minimum(glu, SWIGLU_LIMIT)
            lin = jnp.clip(lin, -SWIGLU_LIMIT, SWIGLU_LIMIT)
            act = glu * jax.nn.sigmoid(SWIGLU_ALPHA * glu) * (lin + 1.0)
            acc = acc + jnp.dot(act.astype(BF16), wd_ref[lo:lo + FF_CHUNK, :], preferred_element_type=F32)
        y_ref[...] = acc

    @pl.when(pl.program_id(0) >= used_ref[0])
    def _():
        y_ref[...] = jnp.zeros((MOE_BM, D_MODEL), F32)


def _experts(blk_exp, used, rows, w_up, b_up, w_down, b_down):
    n_pad = rows.shape[0]
    n_blocks = n_pad // MOE_BM
    live = lambda i, be, us: jnp.minimum(i, us[0] - 1)
    return pl.pallas_call(
        _experts_kernel,
        out_shape=jax.ShapeDtypeStruct((n_pad, D_MODEL), F32),
        grid_spec=pltpu.PrefetchScalarGridSpec(
            num_scalar_prefetch=2, grid=(n_blocks,),
            in_specs=[pl.BlockSpec((MOE_BM, D_MODEL), lambda i, be, us: (live(i, be, us), 0)),
                      pl.BlockSpec((None, D_MODEL, 2 * D_FF), lambda i, be, us: (be[live(i, be, us)], 0, 0)),
                      pl.BlockSpec((None, 1, 2 * D_FF), lambda i, be, us: (be[live(i, be, us)], 0, 0)),
                      pl.BlockSpec((None, D_FF, D_MODEL), lambda i, be, us: (be[live(i, be, us)], 0, 0)),
                      pl.BlockSpec((None, 1, D_MODEL), lambda i, be, us: (be[live(i, be, us)], 0, 0))],
            out_specs=pl.BlockSpec((MOE_BM, D_MODEL), lambda i, be, us: (i, 0))),
        compiler_params=_cparams("arbitrary"),
        name="experts",
    )(blk_exp, used, rows, w_up, b_up, w_down, b_down)


COMBINE_TM = 128


def _combine_kernel(pstart_ref, e_ref, r_ref, xn_ref, gt_ref, mod_ref, fg_ref, y_ref, out_ref, buf_ref, sem):
    tm = COMBINE_TM

    def body(t, carry):
        for k in range(TOP_K):
            src = pstart_ref[e_ref[k, t]] + r_ref[k, t]
            pltpu.make_async_copy(y_ref.at[pl.ds(src, 1)], buf_ref.at[k, pl.ds(t, 1)], sem).start()
        return carry

    lax.fori_loop(0, tm, body, 0)
    for k in range(TOP_K):
        pltpu.make_async_copy(y_ref.at[pl.ds(0, tm)], buf_ref.at[k], sem).wait()

    gt = gt_ref[...]
    y = gt[:, 0:1] * buf_ref[0]
    for k in range(1, TOP_K):
        y = y + gt[:, k:k + 1] * buf_ref[k]
    xo = xn_ref[...] + mod_ref[5:6, :] * y
    out_ref[...] = _rmsnorm_rows(xo, fg_ref[...])


def _combine(pstart, e8, r8, xn, gt, mod8, mod_off, final_g, y_rows, batch, seq):
    tm = COMBINE_TM
    tokens = batch * seq
    tiles_per_row = seq // tm
    smem = pl.BlockSpec((8, tm), lambda i, ps: (0, i), memory_space=pltpu.SMEM)
    return pl.pallas_call(
        _combine_kernel,
        out_shape=jax.ShapeDtypeStruct((tokens, D_MODEL), F32),
        grid_spec=pltpu.PrefetchScalarGridSpec(
            num_scalar_prefetch=1, grid=(tokens // tm,),
            in_specs=[smem, smem,
                      pl.BlockSpec((tm, D_MODEL), lambda i, ps: (i, 0)),
                      pl.BlockSpec((tm, LANES), lambda i, ps: (i, 0)),
                      pl.BlockSpec((None, 8, D_MODEL), lambda i, ps: (mod_off + i // tiles_per_row, 0, 0)),
                      pl.BlockSpec((1, D_MODEL), lambda i, ps: (0, 0)),
                      pl.BlockSpec(memory_space=pl.ANY)],
            out_specs=pl.BlockSpec((tm, D_MODEL), lambda i, ps: (i, 0)),
            scratch_shapes=[pltpu.VMEM((TOP_K, tm, D_MODEL), F32), pltpu.SemaphoreType.DMA(())]),
        compiler_params=_cparams("arbitrary"),
        name="combine",
    )(pstart, e8, r8, xn, gt, mod8, final_g, y_rows)


def _block_plan(counts, n_blocks):
    pcounts = (counts + MOE_BM - 1) // MOE_BM * MOE_BM
    pends = jnp.cumsum(pcounts)
    pstart = (pends - pcounts).astype(jnp.int32)
    used = (pends[-1:] // MOE_BM).astype(jnp.int32)
    blk_start = jnp.arange(n_blocks, dtype=jnp.int32) * MOE_BM
    blk_exp = jnp.minimum(jnp.sum(pends[None, :] <= blk_start[:, None], axis=1), N_EXPERTS - 1)
    return pstart, blk_exp.astype(jnp.int32), used


def _run_group(x, mod8, mod_off, p):
    batch, seq, _ = x.shape
    tokens = batch * seq
    x2 = x.reshape(tokens, D_MODEL)
    q, k, v, z = _in_proj(x2, mod8, mod_off, p["g1"], p["w_in"], _rope_tables(seq), batch, seq, tm=512)
    attn_outs = [_attention_pattern(q, k, v, batch, seq, d) for d in DILATIONS]
    cnt0 = jnp.zeros((N_EXPERTS, LANES), F32)
    xn, h2, e8, r8, gt, cnt = _post(x2, mod8, mod_off, attn_outs, z, p["conv_w"], p["conv_b"], p["ln_g"],
                                    p["ln_b"], p["w_out"], p["g2"], p["w_router2"], p["b_router_t"],
                                    cnt0, batch, seq)
    n_pad = tokens * TOP_K + N_EXPERTS * MOE_BM
    pstart, blk_exp, used = _block_plan(cnt[:, 0].astype(jnp.int32), n_pad // MOE_BM)
    rows = _dispatch(pstart, e8, r8, h2, n_pad)
    y_rows = _experts(blk_exp, used, rows, p["w_up"], p["b_up"], p["w_down"], p["b_down"])
    out = _combine(pstart, e8, r8, xn, gt, mod8, mod_off, p["final_g"], y_rows, batch, seq)
    return out.reshape(batch, seq, D_MODEL)


def _prepare(w_ada, b_ada, norm_mix_g, w_in, conv_w, conv_b, conv_ln_g, conv_ln_b, w_out, norm_ffn_g,
             w_router, b_router, w_up, b_up, w_down, b_down, final_g):
    wr = w_router[0]
    wr_hi = wr.astype(BF16)
    wr_lo = (wr - wr_hi.astype(F32)).astype(BF16)
    w_router2 = jnp.concatenate(
        [wr_hi, wr_lo, jnp.zeros((D_MODEL, ROUTER_W - 2 * N_EXPERTS), BF16)], axis=1)
    return dict(
        g1=norm_mix_g[0][None, :], w_in=w_in[0].astype(BF16),
        conv_w=jnp.concatenate([conv_w[0], jnp.zeros((32 - CONV_K, CONV_W), F32)], axis=0),
        conv_b=conv_b[0][None, :], ln_g=conv_ln_g[0][None, :], ln_b=conv_ln_b[0][None, :],
        w_out=w_out[0].astype(BF16), g2=norm_ffn_g[0][None, :],
        w_router2=w_router2,
        b_router_t=jnp.broadcast_to(b_router[0][:, None], (N_EXPERTS, POST_TM)),
        w_up=w_up[0].astype(BF16), b_up=b_up[0][:, None, :],
        w_down=w_down[0].astype(BF16), b_down=b_down[0][:, None, :],
        final_g=final_g[None, :])


def kernel(x_prompt, x_sample, c_prompt, c_sample, w_ada, b_ada, norm_mix_g, w_in, conv_w, conv_b,
           conv_ln_g, conv_ln_b, w_out, norm_ffn_g, w_router, b_router, w_up, b_up, w_down, b_down,
           final_g):
    p = _prepare(w_ada, b_ada, norm_mix_g, w_in, conv_w, conv_b, conv_ln_g, conv_ln_b, w_out,
                 norm_ffn_g, w_router, b_router, w_up, b_up, w_down, b_down, final_g)
    nb_p, nb_s = c_prompt.shape[0], c_sample.shape[0]
    c_rows = 16
    c_all = jnp.concatenate([c_prompt, c_sample, jnp.zeros((c_rows - nb_p - nb_s, D_MODEL), F32)], axis=0)
    mod = _modulation(c_all, w_ada[0].astype(BF16), b_ada[0][None, :])
    mod8 = jnp.concatenate([mod.reshape(c_rows, 6, D_MODEL), jnp.zeros((c_rows, 2, D_MODEL), F32)], axis=1)
    y_prompt = _run_group(x_prompt, mod8, 0, p)
    y_sample = _run_group(x_sample, mod8, nb_p, p)
    return (y_prompt, y_sample)
```

```python
import functools

import jax
import jax.numpy as jnp
from jax import lax
from jax.experimental import pallas as pl
from jax.experimental.pallas import tpu as pltpu

D_MODEL = 1024
N_HEADS = 8
HEAD_DIM = 64
ATTN_W = N_HEADS * HEAD_DIM
CONV_W = D_MODEL // 2
IN_PROJ_W = 3 * ATTN_W + 2 * CONV_W
WINDOWS = (128, 512, 2048)
DILATIONS = (1, 4, 16)
ATTN_BLOCK = 128
HALF = 64
ROPE_DIM = HEAD_DIM // 4
ROPE_THETA = 500000.0
NEG_INF = -1e30
CONV_K = 31
CONV_PAD = CONV_K // 2
N_EXPERTS = 32
TOP_K = 4
D_FF = D_MODEL
SWIGLU_ALPHA = 1.702
SWIGLU_LIMIT = 7.0
EPS = 1e-6

LANES = 128
VMEM_LIMIT = 56 * 1024 * 1024

F32 = jnp.float32
BF16 = jnp.bfloat16


def _cparams(*sem):
    return pltpu.CompilerParams(dimension_semantics=sem, vmem_limit_bytes=VMEM_LIMIT)


def _mod_kernel(c_ref, w_ref, b_ref, o_ref):
    c = c_ref[...]
    a = (c * jax.nn.sigmoid(c)).astype(BF16)
    o_ref[...] = jnp.dot(a, w_ref[...], preferred_element_type=F32) + b_ref[...]


def _modulation(c_all, w_ada, b_ada):
    rows = c_all.shape[0]
    n = w_ada.shape[1]
    tn = 1536
    return pl.pallas_call(
        _mod_kernel,
        out_shape=jax.ShapeDtypeStruct((rows, n), F32),
        grid=(n // tn,),
        in_specs=[pl.BlockSpec((rows, D_MODEL), lambda j: (0, 0)),
                  pl.BlockSpec((D_MODEL, tn), lambda j: (0, j)),
                  pl.BlockSpec((1, tn), lambda j: (0, j))],
        out_specs=pl.BlockSpec((rows, tn), lambda j: (0, j)),
        compiler_params=_cparams("arbitrary"),
        name="modulation",
    )(c_all, w_ada, b_ada)


def _rmsnorm_rows(x, g):
    return x * lax.rsqrt(jnp.mean(x * x, axis=-1, keepdims=True) + EPS) * g


def _inproj_kernel(x_ref, mod_ref, g_ref, w_ref, cos_ref, s1_ref, s2_ref,
                   q1_ref, k1_ref, v1_ref, q4_ref, k4_ref, v4_ref, q16_ref, k16_ref, v16_ref,
                   z_ref, stage_ref):
    tm = x_ref.shape[0]
    x = x_ref[...]
    shift = mod_ref[0:1, :]
    scale = mod_ref[1:2, :]
    h = _rmsnorm_rows(x, g_ref[...]) * (1.0 + scale) + shift
    hb = h.astype(BF16)
    cos = cos_ref[...]
    s1 = s1_ref[...]
    s2 = s2_ref[...]

    def rope(p):
        return p * cos + pltpu.roll(p, LANES - ROPE_DIM // 2, 1) * s1 + pltpu.roll(p, ROPE_DIM // 2, 1) * s2

    n_chunks = ATTN_W // LANES
    plans = ((0, lambda p: rope(p) * (HEAD_DIM ** -0.5), (q1_ref, q4_ref, q16_ref)),
             (ATTN_W, rope, (k1_ref, k4_ref, k16_ref)),
             (2 * ATTN_W, lambda p: p, (v1_ref, v4_ref, v16_ref)))
    for col0, finish, (nat_ref, *strided_refs) in plans:
        for c in range(n_chunks):
            lo = c * LANES
            val = finish(jnp.dot(hb, w_ref[:, col0 + lo:col0 + lo + LANES], preferred_element_type=F32))
            nat_ref[:, lo:lo + LANES] = val.astype(BF16)
            stage_ref[c] = val
        for d, out_ref in zip(DILATIONS[1:], strided_refs):
            for r in range(d):
                for c in range(n_chunks):
                    lo = r * ATTN_W + c * LANES
                    out_ref[:, lo:lo + LANES] = stage_ref[c, pl.ds(r, tm // d, stride=d), :].astype(BF16)
    a = jnp.dot(hb, w_ref[:, 3 * ATTN_W:3 * ATTN_W + CONV_W], preferred_element_type=F32)
    g = jnp.dot(hb, w_ref[:, 3 * ATTN_W + CONV_W:], preferred_element_type=F32)
    z_ref[...] = (a * jax.nn.sigmoid(g)).astype(BF16)


def _rope_tables(seq):
    half = ROPE_DIM // 2
    inv_freq = ROPE_THETA ** (-jnp.arange(half, dtype=F32) * 2.0 / ROPE_DIM)
    ang = jnp.arange(seq, dtype=F32)[:, None] * inv_freq[None, :]
    cos = jnp.cos(ang)
    sin = jnp.sin(ang)
    ones = jnp.ones((seq, HEAD_DIM - ROPE_DIM), F32)
    zeros = jnp.zeros((seq, HEAD_DIM - ROPE_DIM), F32)
    zh = jnp.zeros((seq, half), F32)
    cos_h = jnp.concatenate([cos, cos, ones], axis=1)
    s1_h = jnp.concatenate([-sin, zh, zeros], axis=1)
    s2_h = jnp.concatenate([zh, sin, zeros], axis=1)
    rep = LANES // HEAD_DIM
    return tuple(jnp.tile(t, (1, rep)) for t in (cos_h, s1_h, s2_h))


def _in_proj(x2, mod8, mod_off, g1, w_in, tables, batch, seq, tm):
    tokens = batch * seq
    tiles_per_row = seq // tm
    row_spec = lambda w: pl.BlockSpec((tm, w), lambda i: (i, 0))
    tab_spec = pl.BlockSpec((tm, LANES), lambda i: (i % tiles_per_row, 0))
    qkv_shapes, qkv_specs = [], []
    for d in DILATIONS:
        qkv_shapes += [jax.ShapeDtypeStruct((tokens // d, d * ATTN_W), BF16)] * 3
        qkv_specs += [pl.BlockSpec((tm // d, d * ATTN_W), lambda i: (i, 0))] * 3
    outs = pl.pallas_call(
        _inproj_kernel,
        out_shape=(*qkv_shapes, jax.ShapeDtypeStruct((tokens, CONV_W), BF16)),
        grid=(tokens // tm,),
        in_specs=[row_spec(D_MODEL),
                  pl.BlockSpec((None, 8, D_MODEL), lambda i: (mod_off + i // tiles_per_row, 0, 0)),
                  pl.BlockSpec((1, D_MODEL), lambda i: (0, 0)),
                  pl.BlockSpec((D_MODEL, IN_PROJ_W), lambda i: (0, 0)),
                  tab_spec, tab_spec, tab_spec],
        out_specs=(*qkv_specs, row_spec(CONV_W)),
        scratch_shapes=[pltpu.VMEM((ATTN_W // LANES, tm, LANES), F32)],
        compiler_params=_cparams("arbitrary"),
        name="in_proj",
    )(x2, mod8, g1, w_in, *tables)
    qkv = [outs[3 * n:3 * n + 3] for n in range(len(DILATIONS))]
    return qkv, outs[-1]


def _attn_kernel(q_ref, kp_ref, kc_ref, kn_ref, vp_ref, vc_ref, vn_ref, o_ref, lse_ref,
                 kw_ref, vw_ref, *, tq, length):
    j = pl.program_id(2)
    kw_ref[0:HALF, :] = kp_ref[...]
    kw_ref[HALF:HALF + tq, :] = kc_ref[...]
    kw_ref[HALF + tq:, :] = kn_ref[...]
    vw_ref[0:HALF, :] = vp_ref[...]
    vw_ref[HALF:HALF + tq, :] = vc_ref[...]
    vw_ref[HALF + tq:, :] = vn_ref[...]

    nk = 2 * ATTN_BLOCK
    t_io = lax.broadcasted_iota(jnp.int32, (ATTN_BLOCK, nk), 0)
    u_io = lax.broadcasted_iota(jnp.int32, (ATTN_BLOCK, nk), 1)
    lane = lax.broadcasted_iota(jnp.int32, (ATTN_BLOCK, LANES), 1)
    first_head = lane < HEAD_DIM
    lane1 = lax.broadcasted_iota(jnp.int32, (1, LANES), 1)
    head_keep = [jnp.where(lane1 < HEAD_DIM, 1.0, 0.0).astype(BF16),
                 jnp.where(lane1 < HEAD_DIM, 0.0, 1.0).astype(BF16)]

    for blk in range(tq // ATTN_BLOCK):
        r0 = blk * ATTN_BLOCK
        base = j * tq + r0
        u_min = jnp.maximum(t_io, HALF - base)
        u_max = jnp.minimum(t_io + 2 * HALF, length + HALF - 1 - base)
        mask = (u_io >= u_min) & (u_io <= u_max)
        lse_tile = jnp.zeros((ATTN_BLOCK, LANES), F32)
        for c in range(ATTN_W // LANES):
            lo = c * LANES
            qc = q_ref[r0:r0 + ATTN_BLOCK, lo:lo + LANES]
            kc = kw_ref[r0:r0 + nk, lo:lo + LANES]
            vc = vw_ref[r0:r0 + nk, lo:lo + LANES]
            outs = []
            for hh in range(LANES // HEAD_DIM):
                qm = qc * head_keep[hh]
                s = lax.dot_general(qm, kc, (((1,), (1,)), ((), ())), preferred_element_type=F32)
                s = jnp.where(mask, s, NEG_INF)
                m = jnp.max(s, axis=-1, keepdims=True)
                p = jnp.exp(s - m)
                den = jnp.sum(p, axis=-1, keepdims=True)
                pv = jnp.dot(p.astype(BF16), vc, preferred_element_type=F32)
                outs.append(pv / den)
                lse_tile = jnp.where(lane == c * (LANES // HEAD_DIM) + hh, m + jnp.log(den), lse_tile)
            o_ref[r0:r0 + ATTN_BLOCK, lo:lo + LANES] = jnp.where(first_head, outs[0], outs[1]).astype(BF16)
        lse_ref[r0:r0 + ATTN_BLOCK, :] = lse_tile


def _attention_pattern(q, k, v, batch, seq, d):
    length = seq // d
    tq = min(length, 512)
    view = lambda t: t.reshape(batch, length, d * ATTN_W)
    nh = tq // HALF
    last_h = length // HALF - 1
    cur = pl.BlockSpec((None, tq, ATTN_W), lambda b, r, j: (b, j, r))
    prev = pl.BlockSpec((None, HALF, ATTN_W), lambda b, r, j: (b, jnp.maximum(j * nh - 1, 0), r))
    nxt = pl.BlockSpec((None, HALF, ATTN_W), lambda b, r, j: (b, jnp.minimum((j + 1) * nh, last_h), r))
    o, lse = pl.pallas_call(
        functools.partial(_attn_kernel, tq=tq, length=length),
        out_shape=(jax.ShapeDtypeStruct((batch, length, d * ATTN_W), BF16),
                   jax.ShapeDtypeStruct((batch, length, d * LANES), F32)),
        grid=(batch, d, length // tq),
        in_specs=[cur, prev, cur, nxt, prev, cur, nxt],
        out_specs=(cur, pl.BlockSpec((None, tq, LANES), lambda b, r, j: (b, j, r))),
        scratch_shapes=[pltpu.VMEM((tq + 2 * HALF, ATTN_W), BF16),
                        pltpu.VMEM((tq + 2 * HALF, ATTN_W), BF16)],
        compiler_params=_cparams("arbitrary", "arbitrary", "arbitrary"),
        name=f"attention_d{d}",
    )(view(q), view(k), view(k), view(k), view(v), view(v), view(v))
    rows = batch * length
    return o.reshape(rows, d * ATTN_W), lse.reshape(rows, d * LANES)


POST_TM = 256
CONV_HALO = 16
CONV_ROWS = 64
ROUTER_W = 128


def _post_kernel(x_ref, mod_ref, o1_ref, o4_ref, o16_ref, l1_ref, l4_ref, l16_ref,
                 zp_ref, zc_ref, zn_ref, cw_ref, cb_ref, lng_ref, lnb_ref,
                 wo_ref, g2_ref, wr_ref, br_ref, cnt0_ref,
                 xn_ref, h2_ref, e_ref, r_ref, gt_ref, cnt_ref,
                 zw_ref, cat_ref, carry_ref, on_ref, ln_ref, *, tiles_per_row):
    tm = POST_TM
    i = pl.program_id(0)

    @pl.when(i == 0)
    def _():
        carry_ref[...] = cnt0_ref[...]

    pos = i % tiles_per_row
    zp = zp_ref[...].astype(F32)
    zn = zn_ref[...].astype(F32)
    zw_ref[0:CONV_HALO, :] = jnp.where(pos == 0, 0.0, zp)
    zw_ref[CONV_HALO:CONV_HALO + tm, :] = zc_ref[...].astype(F32)
    zw_ref[CONV_HALO + tm:, :] = jnp.where(pos == tiles_per_row - 1, 0.0, zn)
    cw = cw_ref[...]
    off = CONV_HALO - CONV_PAD
    for rc in range(tm // CONV_ROWS):
        r0 = rc * CONV_ROWS
        acc = jnp.broadcast_to(cb_ref[...], (CONV_ROWS, CONV_W))
        for tap in range(CONV_K):
            acc = acc + zw_ref[r0 + off + tap:r0 + off + tap + CONV_ROWS, :] * cw[tap:tap + 1, :]
        mu = jnp.mean(acc, axis=-1, keepdims=True)
        cen = acc - mu
        var = jnp.mean(cen * cen, axis=-1, keepdims=True)
        zf = cen * lax.rsqrt(var + EPS) * lng_ref[...] + lnb_ref[...]
        cat_ref[r0:r0 + CONV_ROWS, ATTN_W:] = (zf * jax.nn.sigmoid(zf)).astype(BF16)

    n_chunks = ATTN_W // LANES
    for pi, (d, o_ref, l_ref) in enumerate(((DILATIONS[1], o4_ref, l4_ref), (DILATIONS[2], o16_ref, l16_ref))):
        rows = tm // d
        for r in range(d):
            ln_ref[pi, pl.ds(r, rows, stride=d), :] = l_ref[:, r * LANES:(r + 1) * LANES]
            for c in range(n_chunks):
                lo = r * ATTN_W + c * LANES
                on_ref[pi, c, pl.ds(r, rows, stride=d), :] = o_ref[:, lo:lo + LANES].astype(F32)

    la, lb, lc = l1_ref[...], ln_ref[0], ln_ref[1]
    lmax = jnp.maximum(jnp.maximum(la, lb), lc)
    wa, wb, wc = jnp.exp(la - lmax), jnp.exp(lb - lmax), jnp.exp(lc - lmax)
    inv = 1.0 / (wa + wb + wc)
    wa, wb, wc = wa * inv, wb * inv, wc * inv
    first_head = lax.broadcasted_iota(jnp.int32, (tm, LANES), 1) < HEAD_DIM
    per_chunk = LANES // HEAD_DIM
    for c in range(n_chunks):
        h0 = c * per_chunk
        spread = lambda w: jnp.where(first_head, w[:, h0:h0 + 1], w[:, h0 + 1:h0 + 2])
        lo = c * LANES
        attn = (spread(wa) * o1_ref[:, lo:lo + LANES].astype(F32)
                + spread(wb) * on_ref[0, c] + spread(wc) * on_ref[1, c])
        cat_ref[:, lo:lo + LANES] = attn.astype(BF16)

    mix = jnp.dot(cat_ref[...], wo_ref[...], preferred_element_type=F32)
    gate1 = mod_ref[2:3, :]
    xn = x_ref[...] + gate1 * mix
    xn_ref[...] = xn
    h2 = _rmsnorm_rows(xn, g2_ref[...]) * (1.0 + mod_ref[4:5, :]) + mod_ref[3:4, :]
    h2_ref[...] = h2

    hi = h2.astype(BF16)
    lo = (h2 - hi.astype(F32)).astype(BF16)
    wr = wr_ref[...]
    both = jnp.dot(hi, wr, preferred_element_type=F32) + jnp.dot(lo, wr, preferred_element_type=F32)
    bt = both.T
    logit = bt[0:N_EXPERTS, :] + bt[N_EXPERTS:2 * N_EXPERTS, :] + br_ref[...]

    row = lax.broadcasted_iota(jnp.int32, (N_EXPERTS, tm), 0).astype(F32)
    work = logit
    sel = jnp.zeros((N_EXPERTS, tm), F32)
    idxs, vals, hots = [], [], []
    for _ in range(TOP_K):
        m = jnp.max(work, axis=0, keepdims=True)
        idx = jnp.min(jnp.where(work == m, row, float(N_EXPERTS)), axis=0, keepdims=True)
        hot = row == idx
        work = jnp.where(hot, -jnp.inf, work)
        sel = jnp.where(hot, 1.0, sel)
        idxs.append(idx.astype(jnp.int32))
        vals.append(m)
        hots.append(hot)
    exps = [jnp.exp(v - vals[0]) for v in vals]
    esum = exps[0] + exps[1] + exps[2] + exps[3]
    gates = [e / esum for e in exps]

    tr = lax.broadcasted_iota(jnp.int32, (tm, tm), 0)
    tc = lax.broadcasted_iota(jnp.int32, (tm, tm), 1)
    before = jnp.where(tr < tc, 1.0, 0.0).astype(BF16)
    cnt = jnp.dot(sel.astype(BF16), before, preferred_element_type=F32) + carry_ref[:, 0:1]
    ranks = [jnp.sum(jnp.where(h, cnt, 0.0), axis=0, keepdims=True).astype(jnp.int32) for h in hots]
    carry_ref[...] = carry_ref[...] + jnp.sum(sel, axis=1, keepdims=True)
    cnt_ref[...] = carry_ref[...]

    row8 = lax.broadcasted_iota(jnp.int32, (8, tm), 0)
    e8 = jnp.zeros((8, tm), jnp.int32)
    r8 = jnp.zeros((8, tm), jnp.int32)
    for k in range(TOP_K):
        e8 = jnp.where(row8 == k, idxs[k], e8)
        r8 = jnp.where(row8 == k, ranks[k], r8)
    e_ref[...] = e8
    r_ref[...] = r8
    rowg = lax.broadcasted_iota(jnp.int32, (LANES, tm), 0)
    g_t = jnp.zeros((LANES, tm), F32)
    for k in range(TOP_K):
        g_t = jnp.where(rowg == k, gates[k], g_t)
    gt_ref[...] = g_t.T


def _post(x2, mod8, mod_off, attn_outs, z, conv_w, conv_b, ln_g, ln_b, w_out, g2, w_router2, b_router_t,
          cnt0, batch, seq):
    tm = POST_TM
    tokens = batch * seq
    tiles_per_row = seq // tm
    nh = tm // CONV_HALO
    last_h = tokens // CONV_HALO - 1
    row = lambda w: pl.BlockSpec((tm, w), lambda i: (i, 0))
    full = lambda a, b: pl.BlockSpec((a, b), lambda i: (0, 0))
    (o1, l1), (o4, l4), (o16, l16) = attn_outs
    d4, d16 = DILATIONS[1], DILATIONS[2]
    strided = lambda d, w: pl.BlockSpec((tm // d, d * w), lambda i: (i, 0))
    out_shapes = (jax.ShapeDtypeStruct((tokens, D_MODEL), F32),
                  jax.ShapeDtypeStruct((tokens, D_MODEL), F32),
                  jax.ShapeDtypeStruct((8, tokens), jnp.int32),
                  jax.ShapeDtypeStruct((8, tokens), jnp.int32),
                  jax.ShapeDtypeStruct((tokens, LANES), F32),
                  jax.ShapeDtypeStruct((N_EXPERTS, LANES), F32))
    return pl.pallas_call(
        functools.partial(_post_kernel, tiles_per_row=tiles_per_row),
        out_shape=out_shapes,
        grid=(tokens // tm,),
        in_specs=[row(D_MODEL),
                  pl.BlockSpec((None, 8, D_MODEL), lambda i: (mod_off + i // tiles_per_row, 0, 0)),
                  row(ATTN_W), strided(d4, ATTN_W), strided(d16, ATTN_W),
                  row(LANES), strided(d4, LANES), strided(d16, LANES),
                  pl.BlockSpec((CONV_HALO, CONV_W), lambda i: (jnp.maximum(i * nh - 1, 0), 0)),
                  row(CONV_W),
                  pl.BlockSpec((CONV_HALO, CONV_W), lambda i: (jnp.minimum((i + 1) * nh, last_h), 0)),
                  full(32, CONV_W), full(1, CONV_W), full(1, CONV_W), full(1, CONV_W),
                  full(D_MODEL, D_MODEL), full(1, D_MODEL), full(D_MODEL, ROUTER_W),
                  full(N_EXPERTS, tm), full(N_EXPERTS, LANES)],
        out_specs=(row(D_MODEL), row(D_MODEL),
                   pl.BlockSpec((8, tm), lambda i: (0, i)), pl.BlockSpec((8, tm), lambda i: (0, i)),
                   row(LANES), full(N_EXPERTS, LANES)),
        scratch_shapes=[pltpu.VMEM((tm + 2 * CONV_HALO, CONV_W), F32),
                        pltpu.VMEM((tm, ATTN_W + CONV_W), BF16),
                        pltpu.VMEM((N_EXPERTS, LANES), F32),
                        pltpu.VMEM((2, ATTN_W // LANES, tm, LANES), F32),
                        pltpu.VMEM((2, tm, LANES), F32)],
        compiler_params=_cparams("arbitrary"),
        name="post",
    )(x2, mod8, o1, o4, o16, l1, l4, l16, z, z, z, conv_w, conv_b, ln_g, ln_b, w_out, g2,
      w_router2, b_router_t, cnt0)


DISPATCH_TM = 256
MOE_BM = 256


PLAN_PCOUNT = N_EXPERTS
PLAN_USED = 2 * N_EXPERTS


def _dispatch_kernel(plan_ref, e_ref, r_ref, h_ref, rows_ref, zero_ref, sem, zsem, *, n_blocks):
    tm = DISPATCH_TM

    @pl.when(pl.program_id(0) == 0)
    def _():
        zero_ref[...] = jnp.zeros((MOE_BM, D_MODEL), F32)

        def zero_block(row0):
            return pltpu.make_async_copy(zero_ref, rows_ref.at[pl.ds(row0, MOE_BM)], zsem)

        def fill(e, n):
            pcount = plan_ref[PLAN_PCOUNT + e]

            @pl.when(pcount > 0)
            def _():
                zero_block(pl.multiple_of(plan_ref[e] + pcount - MOE_BM, MOE_BM)).start()

            return n + jnp.where(pcount > 0, 1, 0)

        n_started = lax.fori_loop(0, N_EXPERTS, fill, 0)
        used = plan_ref[PLAN_USED]

        def tail(b, n):
            zero_block(pl.multiple_of(b * MOE_BM, MOE_BM)).start()
            return n + 1

        n_started = lax.fori_loop(used, n_blocks, tail, n_started)

        def drain(_, c):
            zero_block(0).wait()
            return c

        lax.fori_loop(0, n_started, drain, 0)

    def body(t, carry):
        for k in range(TOP_K):
            dest = plan_ref[e_ref[k, t]] + r_ref[k, t]
            pltpu.make_async_copy(h_ref.at[pl.ds(t, 1)], rows_ref.at[pl.ds(dest, 1)], sem).start()
        return carry

    lax.fori_loop(0, tm, body, 0)
    for _ in range(TOP_K):
        pltpu.make_async_copy(h_ref, rows_ref.at[pl.ds(0, tm)], sem).wait()


def _dispatch(plan, e8, r8, h2, n_pad):
    tm = DISPATCH_TM
    tokens = h2.shape[0]
    smem = pl.BlockSpec((8, tm), lambda i, ps: (0, i), memory_space=pltpu.SMEM)
    return pl.pallas_call(
        functools.partial(_dispatch_kernel, n_blocks=n_pad // MOE_BM),
        out_shape=jax.ShapeDtypeStruct((n_pad, D_MODEL), F32),
        grid_spec=pltpu.PrefetchScalarGridSpec(
            num_scalar_prefetch=1, grid=(tokens // tm,),
            in_specs=[smem, smem, pl.BlockSpec((tm, D_MODEL), lambda i, ps: (i, 0))],
            out_specs=pl.BlockSpec(memory_space=pl.ANY),
            scratch_shapes=[pltpu.VMEM((MOE_BM, D_MODEL), F32),
                            pltpu.SemaphoreType.DMA(()), pltpu.SemaphoreType.DMA(())]),
        compiler_params=_cparams("arbitrary"),
        name="dispatch",
    )(plan, e8, r8, h2)


FF_CHUNK = 512


def _experts_kernel(bexp_ref, used_ref, x_ref, wu_ref, bu_ref, wd_ref, bd_ref, y_ref):
    del bexp_ref

    @pl.when(pl.program_id(0) < used_ref[0])
    def _():
        xb = x_ref[...].astype(BF16)
        acc = jnp.broadcast_to(bd_ref[...], (MOE_BM, D_MODEL))
        for c in range(D_FF // FF_CHUNK):
            lo = c * FF_CHUNK
            glu = jnp.dot(xb, wu_ref[:, lo:lo + FF_CHUNK], preferred_element_type=F32) + bu_ref[:, lo:lo + FF_CHUNK]
            lin = (jnp.dot(xb, wu_ref[:, D_FF + lo:D_FF + lo + FF_CHUNK], preferred_element_type=F32)
                   + bu_ref[:, D_FF + lo:D_FF + lo + FF_CHUNK])
            glu = jnp.minimum(glu, SWIGLU_LIMIT)
            lin = jnp.clip(lin, -SWIGLU_LIMIT, SWIGLU_LIMIT)
            act = glu * jax.nn.sigmoid(SWIGLU_ALPHA * glu) * (lin + 1.0)
            acc = acc + jnp.dot(act.astype(BF16), wd_ref[lo:lo + FF_CHUNK, :], preferred_element_type=F32)
        y_ref[...] = acc

    @pl.when(pl.program_id(0) >= used_ref[0])
    def _():
        y_ref[...] = jnp.zeros((MOE_BM, D_MODEL), F32)


def _experts(blk_exp, used, rows, w_up, b_up, w_down, b_down):
    n_pad = rows.shape[0]
    n_blocks = n_pad // MOE_BM
    live = lambda i, be, us: jnp.minimum(i, us[0] - 1)
    return pl.pallas_call(
        _experts_kernel,
        out_shape=jax.ShapeDtypeStruct((n_pad, D_MODEL), F32),
        grid_spec=pltpu.PrefetchScalarGridSpec(
            num_scalar_prefetch=2, grid=(n_blocks,),
            in_specs=[pl.BlockSpec((MOE_BM, D_MODEL), lambda i, be, us: (live(i, be, us), 0)),
                      pl.BlockSpec((None, D_MODEL, 2 * D_FF), lambda i, be, us: (be[live(i, be, us)], 0, 0)),
                      pl.BlockSpec((None, 1, 2 * D_FF), lambda i, be, us: (be[live(i, be, us)], 0, 0)),
                      pl.BlockSpec((None, D_FF, D_MODEL), lambda i, be, us: (be[live(i, be, us)], 0, 0)),
                      pl.BlockSpec((None, 1, D_MODEL), lambda i, be, us: (be[live(i, be, us)], 0, 0))],
            out_specs=pl.BlockSpec((MOE_BM, D_MODEL), lambda i, be, us: (i, 0))),
        compiler_params=_cparams("arbitrary"),
        name="experts",
    )(blk_exp, used, rows, w_up, b_up, w_down, b_down)


COMBINE_TM = 128


def _combine_kernel(pstart_ref, e_ref, r_ref, xn_ref, gt_ref, mod_ref, fg_ref, y_ref, out_ref, buf_ref, sem):
    tm = COMBINE_TM

    def body(t, carry):
        for k in range(TOP_K):
            src = pstart_ref[e_ref[k, t]] + r_ref[k, t]
            pltpu.make_async_copy(y_ref.at[pl.ds(src, 1)], buf_ref.at[k, pl.ds(t, 1)], sem).start()
        return carry

    lax.fori_loop(0, tm, body, 0)
    for k in range(TOP_K):
        pltpu.make_async_copy(y_ref.at[pl.ds(0, tm)], buf_ref.at[k], sem).wait()

    gt = gt_ref[...]
    y = gt[:, 0:1] * buf_ref[0]
    for k in range(1, TOP_K):
        y = y + gt[:, k:k + 1] * buf_ref[k]
    xo = xn_ref[...] + mod_ref[5:6, :] * y
    out_ref[...] = _rmsnorm_rows(xo, fg_ref[...])


def _combine(pstart, e8, r8, xn, gt, mod8, mod_off, final_g, y_rows, batch, seq):
    tm = COMBINE_TM
    tokens = batch * seq
    tiles_per_row = seq // tm
    smem = pl.BlockSpec((8, tm), lambda i, ps: (0, i), memory_space=pltpu.SMEM)
    return pl.pallas_call(
        _combine_kernel,
        out_shape=jax.ShapeDtypeStruct((tokens, D_MODEL), F32),
        grid_spec=pltpu.PrefetchScalarGridSpec(
            num_scalar_prefetch=1, grid=(tokens // tm,),
            in_specs=[smem, smem,
                      pl.BlockSpec((tm, D_MODEL), lambda i, ps: (i, 0)),
                      pl.BlockSpec((tm, LANES), lambda i, ps: (i, 0)),
                      pl.BlockSpec((None, 8, D_MODEL), lambda i, ps: (mod_off + i // tiles_per_row, 0, 0)),
                      pl.BlockSpec((1, D_MODEL), lambda i, ps: (0, 0)),
                      pl.BlockSpec(memory_space=pl.ANY)],
            out_specs=pl.BlockSpec((tm, D_MODEL), lambda i, ps: (i, 0)),
            scratch_shapes=[pltpu.VMEM((TOP_K, tm, D_MODEL), F32), pltpu.SemaphoreType.DMA(())]),
        compiler_params=_cparams("arbitrary"),
        name="combine",
    )(pstart, e8, r8, xn, gt, mod8, final_g, y_rows)


def _block_plan(counts, n_blocks):
    pcounts = (counts + MOE_BM - 1) // MOE_BM * MOE_BM
    pends = jnp.cumsum(pcounts)
    pstart = (pends - pcounts).astype(jnp.int32)
    used = (pends[-1:] // MOE_BM).astype(jnp.int32)
    blk_start = jnp.arange(n_blocks, dtype=jnp.int32) * MOE_BM
    blk_exp = jnp.minimum(jnp.sum(pends[None, :] <= blk_start[:, None], axis=1), N_EXPERTS - 1)
    plan = jnp.concatenate([pstart, pcounts.astype(jnp.int32), used])
    return plan, blk_exp.astype(jnp.int32), used


def _run_group(x, mod8, mod_off, p):
    batch, seq, _ = x.shape
    tokens = batch * seq
    x2 = x.reshape(tokens, D_MODEL)
    qkv, z = _in_proj(x2, mod8, mod_off, p["g1"], p["w_in"], _rope_tables(seq), batch, seq, tm=512)
    attn_outs = [_attention_pattern(*qkv_d, batch, seq, d) for qkv_d, d in zip(qkv, DILATIONS)]
    cnt0 = jnp.zeros((N_EXPERTS, LANES), F32)
    xn, h2, e8, r8, gt, cnt = _post(x2, mod8, mod_off, attn_outs, z, p["conv_w"], p["conv_b"], p["ln_g"],
                                    p["ln_b"], p["w_out"], p["g2"], p["w_router2"], p["b_router_t"],
                                    cnt0, batch, seq)
    n_pad = tokens * TOP_K + N_EXPERTS * MOE_BM
    plan, blk_exp, used = _block_plan(cnt[:, 0].astype(jnp.int32), n_pad // MOE_BM)
    rows = _dispatch(plan, e8, r8, h2, n_pad)
    y_rows = _experts(blk_exp, used, rows, p["w_up"], p["b_up"], p["w_down"], p["b_down"])
    out = _combine(plan, e8, r8, xn, gt, mod8, mod_off, p["final_g"], y_rows, batch, seq)
    return out.reshape(batch, seq, D_MODEL)


def _prepare(w_ada, b_ada, norm_mix_g, w_in, conv_w, conv_b, conv_ln_g, conv_ln_b, w_out, norm_ffn_g,
             w_router, b_router, w_up, b_up, w_down, b_down, final_g):
    wr = w_router[0]
    wr_hi = wr.astype(BF16)
    wr_lo = (wr - wr_hi.astype(F32)).astype(BF16)
    w_router2 = jnp.concatenate(
        [wr_hi, wr_lo, jnp.zeros((D_MODEL, ROUTER_W - 2 * N_EXPERTS), BF16)], axis=1)
    return dict(
        g1=norm_mix_g[0][None, :], w_in=w_in[0].astype(BF16),
        conv_w=jnp.concatenate([conv_w[0], jnp.zeros((32 - CONV_K, CONV_W), F32)], axis=0),
        conv_b=conv_b[0][None, :], ln_g=conv_ln_g[0][None, :], ln_b=conv_ln_b[0][None, :],
        w_out=w_out[0].astype(BF16), g2=norm_ffn_g[0][None, :],
        w_router2=w_router2,
        b_router_t=jnp.broadcast_to(b_router[0][:, None], (N_EXPERTS, POST_TM)),
        w_up=w_up[0].astype(BF16), b_up=b_up[0][:, None, :],
        w_down=w_down[0].astype(BF16), b_down=b_down[0][:, None, :],
        final_g=final_g[None, :])


def kernel(x_prompt, x_sample, c_prompt, c_sample, w_ada, b_ada, norm_mix_g, w_in, conv_w, conv_b,
           conv_ln_g, conv_ln_b, w_out, norm_ffn_g, w_router, b_router, w_up, b_up, w_down, b_down,
           final_g):
    p = _prepare(w_ada, b_ada, norm_mix_g, w_in, conv_w, conv_b, conv_ln_g, conv_ln_b, w_out,
                 norm_ffn_g, w_router, b_router, w_up, b_up, w_down, b_down, final_g)
    nb_p, nb_s = c_prompt.shape[0], c_sample.shape[0]
    c_rows = 16
    c_all = jnp.concatenate([c_prompt, c_sample, jnp.zeros((c_rows - nb_p - nb_s, D_MODEL), F32)], axis=0)
    mod = _modulation(c_all, w_ada[0].astype(BF16), b_ada[0][None, :])
    mod8 = jnp.concatenate([mod.reshape(c_rows, 6, D_MODEL), jnp.zeros((c_rows, 2, D_MODEL), F32)], axis=1)
    y_prompt = _run_group(x_prompt, mod8, 0, p)
    y_sample = _run_group(x_sample, mod8, nb_p, p)
    return (y_prompt, y_sample)
```

```python
import functools

import jax
import jax.numpy as jnp
from jax import lax
from jax.experimental import pallas as pl
from jax.experimental.pallas import tpu as pltpu
from jax.experimental.pallas import tpu_sc as plsc

D_MODEL = 1024
N_HEADS = 8
HEAD_DIM = 64
ATTN_W = N_HEADS * HEAD_DIM
CONV_W = D_MODEL // 2
IN_PROJ_W = 3 * ATTN_W + 2 * CONV_W
WINDOWS = (128, 512, 2048)
DILATIONS = (1, 4, 16)
ATTN_BLOCK = 128
HALF = 64
ROPE_DIM = HEAD_DIM // 4
ROPE_THETA = 500000.0
NEG_INF = -1e30
CONV_K = 31
CONV_PAD = CONV_K // 2
N_EXPERTS = 32
TOP_K = 4
D_FF = D_MODEL
SWIGLU_ALPHA = 1.702
SWIGLU_LIMIT = 7.0
EPS = 1e-6

LANES = 128
VMEM_LIMIT = 56 * 1024 * 1024

F32 = jnp.float32
BF16 = jnp.bfloat16


def _cparams(*sem):
    return pltpu.CompilerParams(dimension_semantics=sem, vmem_limit_bytes=VMEM_LIMIT)


def _mod_kernel(c_ref, w_ref, b_ref, o_ref):
    c = c_ref[...]
    a = (c * jax.nn.sigmoid(c)).astype(BF16)
    o_ref[...] = jnp.dot(a, w_ref[...], preferred_element_type=F32) + b_ref[...]


def _modulation(c_all, w_ada, b_ada):
    rows = c_all.shape[0]
    n = w_ada.shape[1]
    tn = 1536
    return pl.pallas_call(
        _mod_kernel,
        out_shape=jax.ShapeDtypeStruct((rows, n), F32),
        grid=(n // tn,),
        in_specs=[pl.BlockSpec((rows, D_MODEL), lambda j: (0, 0)),
                  pl.BlockSpec((D_MODEL, tn), lambda j: (0, j)),
                  pl.BlockSpec((1, tn), lambda j: (0, j))],
        out_specs=pl.BlockSpec((rows, tn), lambda j: (0, j)),
        compiler_params=_cparams("arbitrary"),
        name="modulation",
    )(c_all, w_ada, b_ada)


def _rmsnorm_rows(x, g):
    return x * lax.rsqrt(jnp.mean(x * x, axis=-1, keepdims=True) + EPS) * g


def _inproj_kernel(x_ref, mod_ref, g_ref, w_ref, cos_ref, s1_ref, s2_ref,
                   q1_ref, k1_ref, v1_ref, q4_ref, k4_ref, v4_ref, q16_ref, k16_ref, v16_ref,
                   z_ref, stage_ref):
    tm = x_ref.shape[0]
    x = x_ref[...]
    shift = mod_ref[0:1, :]
    scale = mod_ref[1:2, :]
    h = _rmsnorm_rows(x, g_ref[...]) * (1.0 + scale) + shift
    hb = h.astype(BF16)
    cos = cos_ref[...]
    s1 = s1_ref[...]
    s2 = s2_ref[...]

    def rope(p):
        return p * cos + pltpu.roll(p, LANES - ROPE_DIM // 2, 1) * s1 + pltpu.roll(p, ROPE_DIM // 2, 1) * s2

    n_chunks = ATTN_W // LANES
    plans = ((0, lambda p: rope(p) * (HEAD_DIM ** -0.5), (q1_ref, q4_ref, q16_ref)),
             (ATTN_W, rope, (k1_ref, k4_ref, k16_ref)),
             (2 * ATTN_W, lambda p: p, (v1_ref, v4_ref, v16_ref)))
    for col0, finish, (nat_ref, *strided_refs) in plans:
        for c in range(n_chunks):
            lo = c * LANES
            val = finish(jnp.dot(hb, w_ref[:, col0 + lo:col0 + lo + LANES], preferred_element_type=F32))
            nat_ref[:, lo:lo + LANES] = val.astype(BF16)
            stage_ref[c] = val
        for d, out_ref in zip(DILATIONS[1:], strided_refs):
            for r in range(d):
                for c in range(n_chunks):
                    lo = r * ATTN_W + c * LANES
                    out_ref[:, lo:lo + LANES] = stage_ref[c, pl.ds(r, tm // d, stride=d), :].astype(BF16)
    a = jnp.dot(hb, w_ref[:, 3 * ATTN_W:3 * ATTN_W + CONV_W], preferred_element_type=F32)
    g = jnp.dot(hb, w_ref[:, 3 * ATTN_W + CONV_W:], preferred_element_type=F32)
    z_ref[...] = (a * jax.nn.sigmoid(g)).astype(BF16)


def _rope_tables(seq):
    half = ROPE_DIM // 2
    inv_freq = ROPE_THETA ** (-jnp.arange(half, dtype=F32) * 2.0 / ROPE_DIM)
    ang = jnp.arange(seq, dtype=F32)[:, None] * inv_freq[None, :]
    cos = jnp.cos(ang)
    sin = jnp.sin(ang)
    ones = jnp.ones((seq, HEAD_DIM - ROPE_DIM), F32)
    zeros = jnp.zeros((seq, HEAD_DIM - ROPE_DIM), F32)
    zh = jnp.zeros((seq, half), F32)
    cos_h = jnp.concatenate([cos, cos, ones], axis=1)
    s1_h = jnp.concatenate([-sin, zh, zeros], axis=1)
    s2_h = jnp.concatenate([zh, sin, zeros], axis=1)
    rep = LANES // HEAD_DIM
    return tuple(jnp.tile(t, (1, rep)) for t in (cos_h, s1_h, s2_h))


def _in_proj(x2, mod8, mod_off, g1, w_in, tables, batch, seq, tm):
    tokens = batch * seq
    tiles_per_row = seq // tm
    row_spec = lambda w: pl.BlockSpec((tm, w), lambda i: (i, 0))
    tab_spec = pl.BlockSpec((tm, LANES), lambda i: (i % tiles_per_row, 0))
    qkv_shapes, qkv_specs = [], []
    for d in DILATIONS:
        qkv_shapes += [jax.ShapeDtypeStruct((tokens // d, d * ATTN_W), BF16)] * 3
        qkv_specs += [pl.BlockSpec((tm // d, d * ATTN_W), lambda i: (i, 0))] * 3
    outs = pl.pallas_call(
        _inproj_kernel,
        out_shape=(*qkv_shapes, jax.ShapeDtypeStruct((tokens, CONV_W), BF16)),
        grid=(tokens // tm,),
        in_specs=[row_spec(D_MODEL),
                  pl.BlockSpec((None, 8, D_MODEL), lambda i: (mod_off + i // tiles_per_row, 0, 0)),
                  pl.BlockSpec((1, D_MODEL), lambda i: (0, 0)),
                  pl.BlockSpec((D_MODEL, IN_PROJ_W), lambda i: (0, 0)),
                  tab_spec, tab_spec, tab_spec],
        out_specs=(*qkv_specs, row_spec(CONV_W)),
        scratch_shapes=[pltpu.VMEM((ATTN_W // LANES, tm, LANES), F32)],
        compiler_params=_cparams("arbitrary"),
        name="in_proj",
    )(x2, mod8, g1, w_in, *tables)
    qkv = [outs[3 * n:3 * n + 3] for n in range(len(DILATIONS))]
    return qkv, outs[-1]


def _attn_kernel(q_ref, kp_ref, kc_ref, kn_ref, vp_ref, vc_ref, vn_ref, o_ref, lse_ref,
                 kw_ref, vw_ref, *, tq, length):
    j = pl.program_id(2)
    kw_ref[0:HALF, :] = kp_ref[...]
    kw_ref[HALF:HALF + tq, :] = kc_ref[...]
    kw_ref[HALF + tq:, :] = kn_ref[...]
    vw_ref[0:HALF, :] = vp_ref[...]
    vw_ref[HALF:HALF + tq, :] = vc_ref[...]
    vw_ref[HALF + tq:, :] = vn_ref[...]

    nk = 2 * ATTN_BLOCK
    t_io = lax.broadcasted_iota(jnp.int32, (ATTN_BLOCK, nk), 0)
    u_io = lax.broadcasted_iota(jnp.int32, (ATTN_BLOCK, nk), 1)
    lane = lax.broadcasted_iota(jnp.int32, (ATTN_BLOCK, LANES), 1)
    first_head = lane < HEAD_DIM
    lane1 = lax.broadcasted_iota(jnp.int32, (1, LANES), 1)
    head_keep = [jnp.where(lane1 < HEAD_DIM, 1.0, 0.0).astype(BF16),
                 jnp.where(lane1 < HEAD_DIM, 0.0, 1.0).astype(BF16)]

    for blk in range(tq // ATTN_BLOCK):
        r0 = blk * ATTN_BLOCK
        base = j * tq + r0
        u_min = jnp.maximum(t_io, HALF - base)
        u_max = jnp.minimum(t_io + 2 * HALF, length + HALF - 1 - base)
        mask = (u_io >= u_min) & (u_io <= u_max)
        lse_tile = jnp.zeros((ATTN_BLOCK, LANES), F32)
        for c in range(ATTN_W // LANES):
            lo = c * LANES
            qc = q_ref[r0:r0 + ATTN_BLOCK, lo:lo + LANES]
            kc = kw_ref[r0:r0 + nk, lo:lo + LANES]
            vc = vw_ref[r0:r0 + nk, lo:lo + LANES]
            outs = []
            for hh in range(LANES // HEAD_DIM):
                qm = qc * head_keep[hh]
                s = lax.dot_general(qm, kc, (((1,), (1,)), ((), ())), preferred_element_type=F32)
                s = jnp.where(mask, s, NEG_INF)
                m = jnp.max(s, axis=-1, keepdims=True)
                p = jnp.exp(s - m)
                den = jnp.sum(p, axis=-1, keepdims=True)
                pv = jnp.dot(p.astype(BF16), vc, preferred_element_type=F32)
                outs.append(pv / den)
                lse_tile = jnp.where(lane == c * (LANES // HEAD_DIM) + hh, m + jnp.log(den), lse_tile)
            o_ref[r0:r0 + ATTN_BLOCK, lo:lo + LANES] = jnp.where(first_head, outs[0], outs[1]).astype(BF16)
        lse_ref[r0:r0 + ATTN_BLOCK, :] = lse_tile


def _attention_pattern(q, k, v, batch, seq, d):
    length = seq // d
    tq = min(length, 512)
    view = lambda t: t.reshape(batch, length, d * ATTN_W)
    nh = tq // HALF
    last_h = length // HALF - 1
    cur = pl.BlockSpec((None, tq, ATTN_W), lambda b, r, j: (b, j, r))
    prev = pl.BlockSpec((None, HALF, ATTN_W), lambda b, r, j: (b, jnp.maximum(j * nh - 1, 0), r))
    nxt = pl.BlockSpec((None, HALF, ATTN_W), lambda b, r, j: (b, jnp.minimum((j + 1) * nh, last_h), r))
    o, lse = pl.pallas_call(
        functools.partial(_attn_kernel, tq=tq, length=length),
        out_shape=(jax.ShapeDtypeStruct((batch, length, d * ATTN_W), BF16),
                   jax.ShapeDtypeStruct((batch, length, d * LANES), F32)),
        grid=(batch, d, length // tq),
        in_specs=[cur, prev, cur, nxt, prev, cur, nxt],
        out_specs=(cur, pl.BlockSpec((None, tq, LANES), lambda b, r, j: (b, j, r))),
        scratch_shapes=[pltpu.VMEM((tq + 2 * HALF, ATTN_W), BF16),
                        pltpu.VMEM((tq + 2 * HALF, ATTN_W), BF16)],
        compiler_params=_cparams("arbitrary", "arbitrary", "arbitrary"),
        name=f"attention_d{d}",
    )(view(q), view(k), view(k), view(k), view(v), view(v), view(v))
    rows = batch * length
    return o.reshape(rows, d * ATTN_W), lse.reshape(rows, d * LANES)


POST_TM = 256
CONV_HALO = 16
CONV_ROWS = 64
ROUTER_W = 128


def _post_kernel(x_ref, mod_ref, o1_ref, o4_ref, o16_ref, l1_ref, l4_ref, l16_ref,
                 zp_ref, zc_ref, zn_ref, cw_ref, cb_ref, lng_ref, lnb_ref,
                 wo_ref, g2_ref, wr_ref, br_ref, cnt0_ref,
                 xn_ref, h2_ref, e_ref, r_ref, gt_ref, cnt_ref,
                 zw_ref, cat_ref, carry_ref, on_ref, ln_ref, *, tiles_per_row):
    tm = POST_TM
    i = pl.program_id(0)

    @pl.when(i == 0)
    def _():
        carry_ref[...] = cnt0_ref[...]

    pos = i % tiles_per_row
    zp = zp_ref[...].astype(F32)
    zn = zn_ref[...].astype(F32)
    zw_ref[0:CONV_HALO, :] = jnp.where(pos == 0, 0.0, zp)
    zw_ref[CONV_HALO:CONV_HALO + tm, :] = zc_ref[...].astype(F32)
    zw_ref[CONV_HALO + tm:, :] = jnp.where(pos == tiles_per_row - 1, 0.0, zn)
    cw = cw_ref[...]
    off = CONV_HALO - CONV_PAD
    for rc in range(tm // CONV_ROWS):
        r0 = rc * CONV_ROWS
        acc = jnp.broadcast_to(cb_ref[...], (CONV_ROWS, CONV_W))
        for tap in range(CONV_K):
            acc = acc + zw_ref[r0 + off + tap:r0 + off + tap + CONV_ROWS, :] * cw[tap:tap + 1, :]
        mu = jnp.mean(acc, axis=-1, keepdims=True)
        cen = acc - mu
        var = jnp.mean(cen * cen, axis=-1, keepdims=True)
        zf = cen * lax.rsqrt(var + EPS) * lng_ref[...] + lnb_ref[...]
        cat_ref[r0:r0 + CONV_ROWS, ATTN_W:] = (zf * jax.nn.sigmoid(zf)).astype(BF16)

    n_chunks = ATTN_W // LANES
    for pi, (d, o_ref, l_ref) in enumerate(((DILATIONS[1], o4_ref, l4_ref), (DILATIONS[2], o16_ref, l16_ref))):
        rows = tm // d
        for r in range(d):
            ln_ref[pi, pl.ds(r, rows, stride=d), :] = l_ref[:, r * LANES:(r + 1) * LANES]
            for c in range(n_chunks):
                lo = r * ATTN_W + c * LANES
                on_ref[pi, c, pl.ds(r, rows, stride=d), :] = o_ref[:, lo:lo + LANES].astype(F32)

    la, lb, lc = l1_ref[...], ln_ref[0], ln_ref[1]
    lmax = jnp.maximum(jnp.maximum(la, lb), lc)
    wa, wb, wc = jnp.exp(la - lmax), jnp.exp(lb - lmax), jnp.exp(lc - lmax)
    inv = 1.0 / (wa + wb + wc)
    wa, wb, wc = wa * inv, wb * inv, wc * inv
    first_head = lax.broadcasted_iota(jnp.int32, (tm, LANES), 1) < HEAD_DIM
    per_chunk = LANES // HEAD_DIM
    for c in range(n_chunks):
        h0 = c * per_chunk
        spread = lambda w: jnp.where(first_head, w[:, h0:h0 + 1], w[:, h0 + 1:h0 + 2])
        lo = c * LANES
        attn = (spread(wa) * o1_ref[:, lo:lo + LANES].astype(F32)
                + spread(wb) * on_ref[0, c] + spread(wc) * on_ref[1, c])
        cat_ref[:, lo:lo + LANES] = attn.astype(BF16)

    mix = jnp.dot(cat_ref[...], wo_ref[...], preferred_element_type=F32)
    gate1 = mod_ref[2:3, :]
    xn = x_ref[...] + gate1 * mix
    xn_ref[...] = xn
    h2 = _rmsnorm_rows(xn, g2_ref[...]) * (1.0 + mod_ref[4:5, :]) + mod_ref[3:4, :]
    h2_ref[...] = h2

    hi = h2.astype(BF16)
    lo = (h2 - hi.astype(F32)).astype(BF16)
    wr = wr_ref[...]
    both = jnp.dot(hi, wr, preferred_element_type=F32) + jnp.dot(lo, wr, preferred_element_type=F32)
    bt = both.T
    logit = bt[0:N_EXPERTS, :] + bt[N_EXPERTS:2 * N_EXPERTS, :] + br_ref[...]

    row = lax.broadcasted_iota(jnp.int32, (N_EXPERTS, tm), 0).astype(F32)
    work = logit
    sel = jnp.zeros((N_EXPERTS, tm), F32)
    idxs, vals, hots = [], [], []
    for _ in range(TOP_K):
        m = jnp.max(work, axis=0, keepdims=True)
        idx = jnp.min(jnp.where(work == m, row, float(N_EXPERTS)), axis=0, keepdims=True)
        hot = row == idx
        work = jnp.where(hot, -jnp.inf, work)
        sel = jnp.where(hot, 1.0, sel)
        idxs.append(idx.astype(jnp.int32))
        vals.append(m)
        hots.append(hot)
    exps = [jnp.exp(v - vals[0]) for v in vals]
    esum = exps[0] + exps[1] + exps[2] + exps[3]
    gates = [e / esum for e in exps]

    tr = lax.broadcasted_iota(jnp.int32, (tm, tm), 0)
    tc = lax.broadcasted_iota(jnp.int32, (tm, tm), 1)
    before = jnp.where(tr < tc, 1.0, 0.0).astype(BF16)
    cnt = jnp.dot(sel.astype(BF16), before, preferred_element_type=F32) + carry_ref[:, 0:1]
    ranks = [jnp.sum(jnp.where(h, cnt, 0.0), axis=0, keepdims=True).astype(jnp.int32) for h in hots]
    carry_ref[...] = carry_ref[...] + jnp.sum(sel, axis=1, keepdims=True)
    cnt_ref[...] = carry_ref[...]

    row8 = lax.broadcasted_iota(jnp.int32, (8, tm), 0)
    e8 = jnp.zeros((8, tm), jnp.int32)
    r8 = jnp.zeros((8, tm), jnp.int32)
    for k in range(TOP_K):
        e8 = jnp.where(row8 == k, idxs[k], e8)
        r8 = jnp.where(row8 == k, ranks[k], r8)
    e_ref[...] = e8
    r_ref[...] = r8
    rowg = lax.broadcasted_iota(jnp.int32, (LANES, tm), 0)
    g_t = jnp.zeros((LANES, tm), F32)
    for k in range(TOP_K):
        g_t = jnp.where(rowg == k, gates[k], g_t)
    gt_ref[...] = g_t.T


def _post(x2, mod8, mod_off, attn_outs, z, conv_w, conv_b, ln_g, ln_b, w_out, g2, w_router2, b_router_t,
          cnt0, batch, seq):
    tm = POST_TM
    tokens = batch * seq
    tiles_per_row = seq // tm
    nh = tm // CONV_HALO
    last_h = tokens // CONV_HALO - 1
    row = lambda w: pl.BlockSpec((tm, w), lambda i: (i, 0))
    full = lambda a, b: pl.BlockSpec((a, b), lambda i: (0, 0))
    (o1, l1), (o4, l4), (o16, l16) = attn_outs
    d4, d16 = DILATIONS[1], DILATIONS[2]
    strided = lambda d, w: pl.BlockSpec((tm // d, d * w), lambda i: (i, 0))
    out_shapes = (jax.ShapeDtypeStruct((tokens, D_MODEL), F32),
                  jax.ShapeDtypeStruct((tokens, D_MODEL), F32),
                  jax.ShapeDtypeStruct((8, tokens), jnp.int32),
                  jax.ShapeDtypeStruct((8, tokens), jnp.int32),
                  jax.ShapeDtypeStruct((tokens, LANES), F32),
                  jax.ShapeDtypeStruct((N_EXPERTS, LANES), F32))
    return pl.pallas_call(
        functools.partial(_post_kernel, tiles_per_row=tiles_per_row),
        out_shape=out_shapes,
        grid=(tokens // tm,),
        in_specs=[row(D_MODEL),
                  pl.BlockSpec((None, 8, D_MODEL), lambda i: (mod_off + i // tiles_per_row, 0, 0)),
                  row(ATTN_W), strided(d4, ATTN_W), strided(d16, ATTN_W),
                  row(LANES), strided(d4, LANES), strided(d16, LANES),
                  pl.BlockSpec((CONV_HALO, CONV_W), lambda i: (jnp.maximum(i * nh - 1, 0), 0)),
                  row(CONV_W),
                  pl.BlockSpec((CONV_HALO, CONV_W), lambda i: (jnp.minimum((i + 1) * nh, last_h), 0)),
                  full(32, CONV_W), full(1, CONV_W), full(1, CONV_W), full(1, CONV_W),
                  full(D_MODEL, D_MODEL), full(1, D_MODEL), full(D_MODEL, ROUTER_W),
                  full(N_EXPERTS, tm), full(N_EXPERTS, LANES)],
        out_specs=(row(D_MODEL), row(D_MODEL),
                   pl.BlockSpec((8, tm), lambda i: (0, i)), pl.BlockSpec((8, tm), lambda i: (0, i)),
                   row(LANES), full(N_EXPERTS, LANES)),
        scratch_shapes=[pltpu.VMEM((tm + 2 * CONV_HALO, CONV_W), F32),
                        pltpu.VMEM((tm, ATTN_W + CONV_W), BF16),
                        pltpu.VMEM((N_EXPERTS, LANES), F32),
                        pltpu.VMEM((2, ATTN_W // LANES, tm, LANES), F32),
                        pltpu.VMEM((2, tm, LANES), F32)],
        compiler_params=_cparams("arbitrary"),
        name="post",
    )(x2, mod8, o1, o4, o16, l1, l4, l16, z, z, z, conv_w, conv_b, ln_g, ln_b, w_out, g2,
      w_router2, b_router_t, cnt0)


MOE_BM = 256

SLOT_TL = 2048
SC_CORES = 2
SC_SUBCORES = 16
SC_WORKERS = SC_CORES * SC_SUBCORES
SC_CHUNK = 32


def _slot_kernel(plan_ref, e_ref, r_ref, s_ref):
    e = e_ref[...]
    start = jnp.zeros(e.shape, jnp.int32)
    for x in range(N_EXPERTS):
        start = jnp.where(e == x, plan_ref[x], start)
    s_ref[...] = start + r_ref[...]


def _slots(plan, e8, r8):
    tokens = e8.shape[1]
    spec = pl.BlockSpec((8, SLOT_TL), lambda i, ps: (0, i))
    s8 = pl.pallas_call(
        _slot_kernel,
        out_shape=jax.ShapeDtypeStruct((8, tokens), jnp.int32),
        grid_spec=pltpu.PrefetchScalarGridSpec(
            num_scalar_prefetch=1, grid=(tokens // SLOT_TL,), in_specs=[spec, spec], out_specs=spec),
        compiler_params=_cparams("arbitrary"),
        name="slots",
    )(plan, e8, r8)
    return s8[:TOP_K].reshape(TOP_K, tokens // SC_CHUNK, SC_CHUNK).transpose(1, 0, 2)


def _sc_worker_chunks(tokens):
    return tokens // (SC_WORKERS * SC_CHUNK)


def _sc_mesh():
    return plsc.VectorSubcoreMesh(core_axis_name="c", subcore_axis_name="s")


def _dispatch(slots, h2, n_pad):
    tokens = h2.shape[0]
    n_chunks = _sc_worker_chunks(tokens)

    @functools.partial(
        pl.kernel, mesh=_sc_mesh(),
        out_type=jax.ShapeDtypeStruct((n_pad, D_MODEL), F32),
        scratch_types=[pltpu.VMEM((TOP_K, SC_CHUNK), jnp.int32),
                       pltpu.VMEM((SC_CHUNK, D_MODEL), F32),
                       pltpu.SemaphoreType.DMA],
        name="sc_dispatch")
    def scatter(h_hbm, slot_hbm, rows_hbm, idx_v, rows_v, sem):
        worker = lax.axis_index("s") * SC_CORES + lax.axis_index("c")

        @pl.loop(0, n_chunks)
        def _(ci):
            chunk = worker * n_chunks + ci
            pltpu.sync_copy(slot_hbm.at[chunk], idx_v)
            pltpu.sync_copy(h_hbm.at[pl.ds(chunk * SC_CHUNK, SC_CHUNK)], rows_v)
            copies = [pltpu.async_copy(rows_v, rows_hbm.at[idx_v.at[k]], sem) for k in range(TOP_K)]
            for cp in copies:
                cp.wait()

    return scatter(h2, slots)


FF_CHUNK = 512


def _experts_kernel(bexp_ref, used_ref, x_ref, wu_ref, bu_ref, wd_ref, bd_ref, y_ref):
    del bexp_ref

    @pl.when(pl.program_id(0) < used_ref[0])
    def _():
        xb = x_ref[...].astype(BF16)
        acc = jnp.broadcast_to(bd_ref[...], (MOE_BM, D_MODEL))
        for c in range(D_FF // FF_CHUNK):
            lo = c * FF_CHUNK
            glu = jnp.dot(xb, wu_ref[:, lo:lo + FF_CHUNK], preferred_element_type=F32) + bu_ref[:, lo:lo + FF_CHUNK]
            lin = (jnp.dot(xb, wu_ref[:, D_FF + lo:D_FF + lo + FF_CHUNK], preferred_element_type=F32)
                   + bu_ref[:, D_FF + lo:D_FF + lo + FF_CHUNK])
            glu = jnp.minimum(glu, SWIGLU_LIMIT)
            lin = jnp.clip(lin, -SWIGLU_LIMIT, SWIGLU_LIMIT)
            act = glu * jax.nn.sigmoid(SWIGLU_ALPHA * glu) * (lin + 1.0)
            acc = acc + jnp.dot(act.astype(BF16), wd_ref[lo:lo + FF_CHUNK, :], preferred_element_type=F32)
        y_ref[...] = acc

    @pl.when(pl.program_id(0) >= used_ref[0])
    def _():
        y_ref[...] = jnp.zeros((MOE_BM, D_MODEL), F32)


def _experts(blk_exp, used, rows, w_up, b_up, w_down, b_down):
    n_pad = rows.shape[0]
    n_blocks = n_pad // MOE_BM
    live = lambda i, be, us: jnp.minimum(i, us[0] - 1)
    return pl.pallas_call(
        _experts_kernel,
        out_shape=jax.ShapeDtypeStruct((n_pad, D_MODEL), F32),
        grid_spec=pltpu.PrefetchScalarGridSpec(
            num_scalar_prefetch=2, grid=(n_blocks,),
            in_specs=[pl.BlockSpec((MOE_BM, D_MODEL), lambda i, be, us: (live(i, be, us), 0)),
                      pl.BlockSpec((None, D_MODEL, 2 * D_FF), lambda i, be, us: (be[live(i, be, us)], 0, 0)),
                      pl.BlockSpec((None, 1, 2 * D_FF), lambda i, be, us: (be[live(i, be, us)], 0, 0)),
                      pl.BlockSpec((None, D_FF, D_MODEL), lambda i, be, us: (be[live(i, be, us)], 0, 0)),
                      pl.BlockSpec((None, 1, D_MODEL), lambda i, be, us: (be[live(i, be, us)], 0, 0))],
            out_specs=pl.BlockSpec((MOE_BM, D_MODEL), lambda i, be, us: (i, 0))),
        compiler_params=_cparams("arbitrary"),
        name="experts",
    )(blk_exp, used, rows, w_up, b_up, w_down, b_down)


COMBINE_TM = 256


def _gather_rows(slots, y_rows, tokens):
    n_chunks = _sc_worker_chunks(tokens)

    @functools.partial(
        pl.kernel, mesh=_sc_mesh(),
        out_type=jax.ShapeDtypeStruct((TOP_K, tokens, D_MODEL), F32),
        scratch_types=[pltpu.VMEM((TOP_K, SC_CHUNK), jnp.int32),
                       pltpu.VMEM((SC_CHUNK, D_MODEL), F32),
                       pltpu.SemaphoreType.DMA],
        name="sc_gather")
    def gather(y_hbm, slot_hbm, out_hbm, idx_v, rows_v, sem):
        worker = lax.axis_index("s") * SC_CORES + lax.axis_index("c")

        @pl.loop(0, n_chunks)
        def _(ci):
            chunk = worker * n_chunks + ci
            pltpu.sync_copy(slot_hbm.at[chunk], idx_v)
            for k in range(TOP_K):
                pltpu.async_copy(y_hbm.at[idx_v.at[k]], rows_v, sem).wait()
                pltpu.sync_copy(rows_v, out_hbm.at[k, pl.ds(chunk * SC_CHUNK, SC_CHUNK)])

    return gather(y_rows, slots)


def _combine_kernel(xn_ref, gt_ref, mod_ref, fg_ref, rows_ref, out_ref):
    gt = gt_ref[...]
    y = gt[:, 0:1] * rows_ref[0]
    for k in range(1, TOP_K):
        y = y + gt[:, k:k + 1] * rows_ref[k]
    xo = xn_ref[...] + mod_ref[5:6, :] * y
    out_ref[...] = _rmsnorm_rows(xo, fg_ref[...])


def _combine(xn, gt, mod8, mod_off, final_g, gathered, batch, seq):
    tm = COMBINE_TM
    tokens = batch * seq
    tiles_per_row = seq // tm
    return pl.pallas_call(
        _combine_kernel,
        out_shape=jax.ShapeDtypeStruct((tokens, D_MODEL), F32),
        grid=(tokens // tm,),
        in_specs=[pl.BlockSpec((tm, D_MODEL), lambda i: (i, 0)),
                  pl.BlockSpec((tm, LANES), lambda i: (i, 0)),
                  pl.BlockSpec((None, 8, D_MODEL), lambda i: (mod_off + i // tiles_per_row, 0, 0)),
                  pl.BlockSpec((1, D_MODEL), lambda i: (0, 0)),
                  pl.BlockSpec((TOP_K, tm, D_MODEL), lambda i: (0, i, 0))],
        out_specs=pl.BlockSpec((tm, D_MODEL), lambda i: (i, 0)),
        compiler_params=_cparams("arbitrary"),
        name="combine",
    )(xn, gt, mod8, final_g, gathered)


def _block_plan(counts, n_blocks):
    pcounts = (counts + MOE_BM - 1) // MOE_BM * MOE_BM
    pends = jnp.cumsum(pcounts)
    pstart = (pends - pcounts).astype(jnp.int32)
    used = (pends[-1:] // MOE_BM).astype(jnp.int32)
    blk_start = jnp.arange(n_blocks, dtype=jnp.int32) * MOE_BM
    blk_exp = jnp.minimum(jnp.sum(pends[None, :] <= blk_start[:, None], axis=1), N_EXPERTS - 1)
    return pstart, blk_exp.astype(jnp.int32), used


def _run_group(x, mod8, mod_off, p):
    batch, seq, _ = x.shape
    tokens = batch * seq
    x2 = x.reshape(tokens, D_MODEL)
    qkv, z = _in_proj(x2, mod8, mod_off, p["g1"], p["w_in"], _rope_tables(seq), batch, seq, tm=512)
    attn_outs = [_attention_pattern(*qkv_d, batch, seq, d) for qkv_d, d in zip(qkv, DILATIONS)]
    cnt0 = jnp.zeros((N_EXPERTS, LANES), F32)
    xn, h2, e8, r8, gt, cnt = _post(x2, mod8, mod_off, attn_outs, z, p["conv_w"], p["conv_b"], p["ln_g"],
                                    p["ln_b"], p["w_out"], p["g2"], p["w_router2"], p["b_router_t"],
                                    cnt0, batch, seq)
    n_pad = tokens * TOP_K + N_EXPERTS * MOE_BM
    plan, blk_exp, used = _block_plan(cnt[:, 0].astype(jnp.int32), n_pad // MOE_BM)
    slots = _slots(plan, e8, r8)
    rows = _dispatch(slots, h2, n_pad)
    y_rows = _experts(blk_exp, used, rows, p["w_up"], p["b_up"], p["w_down"], p["b_down"])
    gathered = _gather_rows(slots, y_rows, tokens)
    out = _combine(xn, gt, mod8, mod_off, p["final_g"], gathered, batch, seq)
    return out.reshape(batch, seq, D_MODEL)


def _prepare(w_ada, b_ada, norm_mix_g, w_in, conv_w, conv_b, conv_ln_g, conv_ln_b, w_out, norm_ffn_g,
             w_router, b_router, w_up, b_up, w_down, b_down, final_g):
    wr = w_router[0]
    wr_hi = wr.astype(BF16)
    wr_lo = (wr - wr_hi.astype(F32)).astype(BF16)
    w_router2 = jnp.concatenate(
        [wr_hi, wr_lo, jnp.zeros((D_MODEL, ROUTER_W - 2 * N_EXPERTS), BF16)], axis=1)
    return dict(
        g1=norm_mix_g[0][None, :], w_in=w_in[0].astype(BF16),
        conv_w=jnp.concatenate([conv_w[0], jnp.zeros((32 - CONV_K, CONV_W), F32)], axis=0),
        conv_b=conv_b[0][None, :], ln_g=conv_ln_g[0][None, :], ln_b=conv_ln_b[0][None, :],
        w_out=w_out[0].astype(BF16), g2=norm_ffn_g[0][None, :],
        w_router2=w_router2,
        b_router_t=jnp.broadcast_to(b_router[0][:, None], (N_EXPERTS, POST_TM)),
        w_up=w_up[0].astype(BF16), b_up=b_up[0][:, None, :],
        w_down=w_down[0].astype(BF16), b_down=b_down[0][:, None, :],
        final_g=final_g[None, :])


def kernel(x_prompt, x_sample, c_prompt, c_sample, w_ada, b_ada, norm_mix_g, w_in, conv_w, conv_b,
           conv_ln_g, conv_ln_b, w_out, norm_ffn_g, w_router, b_router, w_up, b_up, w_down, b_down,
           final_g):
    p = _prepare(w_ada, b_ada, norm_mix_g, w_in, conv_w, conv_b, conv_ln_g, conv_ln_b, w_out,
                 norm_ffn_g, w_router, b_router, w_up, b_up, w_down, b_down, final_g)
    nb_p, nb_s = c_prompt.shape[0], c_sample.shape[0]
    c_rows = 16
    c_all = jnp.concatenate([c_prompt, c_sample, jnp.zeros((c_rows - nb_p - nb_s, D_MODEL), F32)], axis=0)
    mod = _modulation(c_all, w_ada[0].astype(BF16), b_ada[0][None, :])
    mod8 = jnp.concatenate([mod.reshape(c_rows, 6, D_MODEL), jnp.zeros((c_rows, 2, D_MODEL), F32)], axis=1)
    y_prompt = _run_group(x_prompt, mod8, 0, p)
    y_sample = _run_group(x_sample, mod8, nb_p, p)
    return (y_prompt, y_sample)
```

```python
import functools

import jax
import jax.numpy as jnp
from jax import lax
from jax.experimental import pallas as pl
from jax.experimental.pallas import tpu as pltpu
from jax.experimental.pallas import tpu_sc as plsc

D_MODEL = 1024
N_HEADS = 8
HEAD_DIM = 64
ATTN_W = N_HEADS * HEAD_DIM
CONV_W = D_MODEL // 2
IN_PROJ_W = 3 * ATTN_W + 2 * CONV_W
WINDOWS = (128, 512, 2048)
DILATIONS = (1, 4, 16)
ATTN_BLOCK = 128
HALF = 64
ROPE_DIM = HEAD_DIM // 4
ROPE_THETA = 500000.0
NEG_INF = -1e30
CONV_K = 31
CONV_PAD = CONV_K // 2
N_EXPERTS = 32
TOP_K = 4
D_FF = D_MODEL
SWIGLU_ALPHA = 1.702
SWIGLU_LIMIT = 7.0
EPS = 1e-6

LANES = 128
SUBLANES = 8
VMEM_LIMIT = 56 * 1024 * 1024

F32 = jnp.float32
BF16 = jnp.bfloat16


def _cparams(*sem):
    return pltpu.CompilerParams(dimension_semantics=sem, vmem_limit_bytes=VMEM_LIMIT)


def _mod_kernel(c_ref, w_ref, b_ref, o_ref):
    c = c_ref[...]
    a = (c * jax.nn.sigmoid(c)).astype(BF16)
    o_ref[...] = jnp.dot(a, w_ref[...], preferred_element_type=F32) + b_ref[...]


def _modulation(c_all, w_ada, b_ada):
    rows = c_all.shape[0]
    n = w_ada.shape[1]
    tn = 1536
    return pl.pallas_call(
        _mod_kernel,
        out_shape=jax.ShapeDtypeStruct((rows, n), F32),
        grid=(n // tn,),
        in_specs=[pl.BlockSpec((rows, D_MODEL), lambda j: (0, 0)),
                  pl.BlockSpec((D_MODEL, tn), lambda j: (0, j)),
                  pl.BlockSpec((1, tn), lambda j: (0, j))],
        out_specs=pl.BlockSpec((rows, tn), lambda j: (0, j)),
        compiler_params=_cparams("arbitrary"),
        name="modulation",
    )(c_all, w_ada, b_ada)


def _rmsnorm_rows(x, g):
    return x * lax.rsqrt(jnp.mean(x * x, axis=-1, keepdims=True) + EPS) * g


def _pack_halves(x):
    n = x.shape[1] // 2
    lo = lax.bitcast_convert_type(x[:, :n], jnp.uint32)
    hi = lax.bitcast_convert_type(x[:, n:], jnp.uint32)
    return (lo >> 16) | (hi & jnp.uint32(0xFFFF0000))


def _unpack_halves(p):
    lo = lax.bitcast_convert_type(p << 16, F32)
    hi = lax.bitcast_convert_type(p & jnp.uint32(0xFFFF0000), F32)
    return lo, hi


def _inproj_kernel(x_ref, mod_ref, g_ref, w_ref, cos_ref, s1_ref, s2_ref,
                   q1_ref, k1_ref, v1_ref, q4_ref, k4_ref, v4_ref, q16_ref, k16_ref, v16_ref,
                   z_ref, stage_ref):
    tm = x_ref.shape[0]
    x = x_ref[...]
    shift = mod_ref[0:1, :]
    scale = mod_ref[1:2, :]
    h = _rmsnorm_rows(x, g_ref[...]) * (1.0 + scale) + shift
    hb = h.astype(BF16)
    cos = cos_ref[...]
    s1 = s1_ref[...]
    s2 = s2_ref[...]

    def rope(p):
        return p * cos + pltpu.roll(p, LANES - ROPE_DIM // 2, 1) * s1 + pltpu.roll(p, ROPE_DIM // 2, 1) * s2

    n_chunks = ATTN_W // LANES
    plans = ((0, lambda p: rope(p) * (HEAD_DIM ** -0.5), (q1_ref, q4_ref, q16_ref)),
             (ATTN_W, rope, (k1_ref, k4_ref, k16_ref)),
             (2 * ATTN_W, lambda p: p, (v1_ref, v4_ref, v16_ref)))
    for col0, finish, (nat_ref, *strided_refs) in plans:
        proj = jnp.dot(hb, w_ref[:, col0:col0 + ATTN_W], preferred_element_type=F32)
        for c in range(n_chunks):
            lo = c * LANES
            val = finish(proj[:, lo:lo + LANES])
            nat_ref[:, lo:lo + LANES] = val.astype(BF16)
            stage_ref[c] = val
        for d, out_ref in zip(DILATIONS[1:], strided_refs):
            for r in range(d):
                for c in range(n_chunks):
                    lo = r * ATTN_W + c * LANES
                    out_ref[:, lo:lo + LANES] = stage_ref[c, pl.ds(r, tm // d, stride=d), :].astype(BF16)
    a = jnp.dot(hb, w_ref[:, 3 * ATTN_W:3 * ATTN_W + CONV_W], preferred_element_type=F32)
    g = jnp.dot(hb, w_ref[:, 3 * ATTN_W + CONV_W:], preferred_element_type=F32)
    z_ref[...] = (a * jax.nn.sigmoid(g)).astype(BF16)


def _rope_tables(seq):
    half = ROPE_DIM // 2
    inv_freq = ROPE_THETA ** (-jnp.arange(half, dtype=F32) * 2.0 / ROPE_DIM)
    ang = jnp.arange(seq, dtype=F32)[:, None] * inv_freq[None, :]
    cos = jnp.cos(ang)
    sin = jnp.sin(ang)
    ones = jnp.ones((seq, HEAD_DIM - ROPE_DIM), F32)
    zeros = jnp.zeros((seq, HEAD_DIM - ROPE_DIM), F32)
    zh = jnp.zeros((seq, half), F32)
    cos_h = jnp.concatenate([cos, cos, ones], axis=1)
    s1_h = jnp.concatenate([-sin, zh, zeros], axis=1)
    s2_h = jnp.concatenate([zh, sin, zeros], axis=1)
    rep = LANES // HEAD_DIM
    return tuple(jnp.tile(t, (1, rep)) for t in (cos_h, s1_h, s2_h))


def _in_proj(x2, mod8, mod_off, g1, w_in, tables, batch, seq, tm):
    tokens = batch * seq
    tiles_per_row = seq // tm
    row_spec = lambda w: pl.BlockSpec((tm, w), lambda i: (i, 0))
    tab_spec = pl.BlockSpec((tm, LANES), lambda i: (i % tiles_per_row, 0))
    qkv_shapes, qkv_specs = [], []
    for d in DILATIONS:
        qkv_shapes += [jax.ShapeDtypeStruct((tokens // d, d * ATTN_W), BF16)] * 3
        qkv_specs += [pl.BlockSpec((tm // d, d * ATTN_W), lambda i: (i, 0))] * 3
    outs = pl.pallas_call(
        _inproj_kernel,
        out_shape=(*qkv_shapes, jax.ShapeDtypeStruct((tokens, CONV_W), BF16)),
        grid=(tokens // tm,),
        in_specs=[row_spec(D_MODEL),
                  pl.BlockSpec((None, 8, D_MODEL), lambda i: (mod_off + i // tiles_per_row, 0, 0)),
                  pl.BlockSpec((1, D_MODEL), lambda i: (0, 0)),
                  pl.BlockSpec((D_MODEL, IN_PROJ_W), lambda i: (0, 0)),
                  tab_spec, tab_spec, tab_spec],
        out_specs=(*qkv_specs, row_spec(CONV_W)),
        scratch_shapes=[pltpu.VMEM((ATTN_W // LANES, tm, LANES), F32)],
        compiler_params=_cparams("arbitrary"),
        name="in_proj",
    )(x2, mod8, g1, w_in, *tables)
    qkv = [outs[3 * n:3 * n + 3] for n in range(len(DILATIONS))]
    return qkv, outs[-1]


def _attn_kernel(q_ref, kp_ref, kc_ref, kn_ref, vp_ref, vc_ref, vn_ref, o_ref, lse_ref,
                 kw_ref, vw_ref, *, tq, length):
    j = pl.program_id(2)
    kw_ref[0:HALF, :] = kp_ref[...]
    kw_ref[HALF:HALF + tq, :] = kc_ref[...]
    kw_ref[HALF + tq:, :] = kn_ref[...]
    vw_ref[0:HALF, :] = vp_ref[...]
    vw_ref[HALF:HALF + tq, :] = vc_ref[...]
    vw_ref[HALF + tq:, :] = vn_ref[...]

    nk = 2 * ATTN_BLOCK
    t_io = lax.broadcasted_iota(jnp.int32, (ATTN_BLOCK, nk), 0)
    u_io = lax.broadcasted_iota(jnp.int32, (ATTN_BLOCK, nk), 1)
    lane = lax.broadcasted_iota(jnp.int32, (ATTN_BLOCK, LANES), 1)
    first_head = lane < HEAD_DIM
    lane1 = lax.broadcasted_iota(jnp.int32, (1, LANES), 1)
    head_keep = [jnp.where(lane1 < HEAD_DIM, 1.0, 0.0).astype(BF16),
                 jnp.where(lane1 < HEAD_DIM, 0.0, 1.0).astype(BF16)]

    for blk in range(tq // ATTN_BLOCK):
        r0 = blk * ATTN_BLOCK
        base = j * tq + r0
        u_min = jnp.maximum(t_io, HALF - base)
        u_max = jnp.minimum(t_io + 2 * HALF, length + HALF - 1 - base)
        mask = (u_io >= u_min) & (u_io <= u_max)
        lse_tile = jnp.zeros((ATTN_BLOCK, LANES), F32)
        for c in range(ATTN_W // LANES):
            lo = c * LANES
            qc = q_ref[r0:r0 + ATTN_BLOCK, lo:lo + LANES]
            kc = kw_ref[r0:r0 + nk, lo:lo + LANES]
            vc = vw_ref[r0:r0 + nk, lo:lo + LANES]
            outs = []
            for hh in range(LANES // HEAD_DIM):
                qm = qc * head_keep[hh]
                s = lax.dot_general(qm, kc, (((1,), (1,)), ((), ())), preferred_element_type=F32)
                s = jnp.where(mask, s, NEG_INF)
                m = jnp.max(s, axis=-1, keepdims=True)
                p = jnp.exp(s - m)
                den = jnp.sum(p, axis=-1, keepdims=True)
                pv = jnp.dot(p.astype(BF16), vc, preferred_element_type=F32)
                outs.append(pv / den)
                lse_tile = jnp.where(lane == c * (LANES // HEAD_DIM) + hh, m + jnp.log(den), lse_tile)
            o_ref[r0:r0 + ATTN_BLOCK, lo:lo + LANES] = jnp.where(first_head, outs[0], outs[1]).astype(BF16)
        lse_ref[r0:r0 + ATTN_BLOCK, :] = lse_tile


def _attention_pattern(q, k, v, batch, seq, d):
    length = seq // d
    tq = min(length, 512)
    view = lambda t: t.reshape(batch, length, d * ATTN_W)
    nh = tq // HALF
    last_h = length // HALF - 1
    cur = pl.BlockSpec((None, tq, ATTN_W), lambda b, r, j: (b, j, r))
    prev = pl.BlockSpec((None, HALF, ATTN_W), lambda b, r, j: (b, jnp.maximum(j * nh - 1, 0), r))
    nxt = pl.BlockSpec((None, HALF, ATTN_W), lambda b, r, j: (b, jnp.minimum((j + 1) * nh, last_h), r))
    o, lse = pl.pallas_call(
        functools.partial(_attn_kernel, tq=tq, length=length),
        out_shape=(jax.ShapeDtypeStruct((batch, length, d * ATTN_W), BF16),
                   jax.ShapeDtypeStruct((batch, length, d * LANES), F32)),
        grid=(batch, d, length // tq),
        in_specs=[cur, prev, cur, nxt, prev, cur, nxt],
        out_specs=(cur, pl.BlockSpec((None, tq, LANES), lambda b, r, j: (b, j, r))),
        scratch_shapes=[pltpu.VMEM((tq + 2 * HALF, ATTN_W), BF16),
                        pltpu.VMEM((tq + 2 * HALF, ATTN_W), BF16)],
        compiler_params=_cparams("arbitrary", "arbitrary", "arbitrary"),
        name=f"attention_d{d}",
    )(view(q), view(k), view(k), view(k), view(v), view(v), view(v))
    rows = batch * length
    return o.reshape(rows, d * ATTN_W), lse.reshape(rows, d * LANES)


POST_TM = 256
CONV_HALO = 16
CONV_ROWS = 64
ROUTER_W = 128
ROW_WORDS = D_MODEL // 2


def _post_kernel(x_ref, mod_ref, o1_ref, o4_ref, o16_ref, l1_ref, l4_ref, l16_ref,
                 zp_ref, zc_ref, zn_ref, cw_ref, cb_ref, lng_ref, lnb_ref,
                 wo_ref, g2_ref, wr_ref, br_ref, cnt0_ref,
                 xn_ref, h2_ref, e_ref, r_ref, gt_ref, cnt_ref,
                 zw_ref, zs_ref, cat_ref, carry_ref, on_ref, ln_ref, *, tiles_per_row):
    tm = POST_TM
    i = pl.program_id(0)

    @pl.when(i == 0)
    def _():
        carry_ref[...] = cnt0_ref[...]

    pos = i % tiles_per_row
    zp = zp_ref[...].astype(F32)
    zn = zn_ref[...].astype(F32)
    zw_ref[0:CONV_HALO, :] = jnp.where(pos == 0, 0.0, zp)
    zw_ref[CONV_HALO:CONV_HALO + tm, :] = zc_ref[...].astype(F32)
    zw_ref[CONV_HALO + tm:, :] = jnp.where(pos == tiles_per_row - 1, 0.0, zn)
    cw = cw_ref[...]
    off = CONV_HALO - CONV_PAD
    span = tm + 2 * CONV_HALO - SUBLANES
    for s in range(SUBLANES):
        zs_ref[s, 0:span, :] = zw_ref[s:s + span, :]
    for rc in range(tm // CONV_ROWS):
        r0 = rc * CONV_ROWS
        acc = jnp.broadcast_to(cb_ref[...], (CONV_ROWS, CONV_W))
        for tap in range(CONV_K):
            q, s = divmod(off + tap, SUBLANES)
            lo = r0 + q * SUBLANES
            acc = acc + zs_ref[s, lo:lo + CONV_ROWS, :] * cw[tap:tap + 1, :]
        mu = jnp.mean(acc, axis=-1, keepdims=True)
        cen = acc - mu
        var = jnp.mean(cen * cen, axis=-1, keepdims=True)
        zf = cen * lax.rsqrt(var + EPS) * lng_ref[...] + lnb_ref[...]
        cat_ref[r0:r0 + CONV_ROWS, ATTN_W:] = (zf * jax.nn.sigmoid(zf)).astype(BF16)

    n_chunks = ATTN_W // LANES
    for pi, (d, o_ref, l_ref) in enumerate(((DILATIONS[1], o4_ref, l4_ref), (DILATIONS[2], o16_ref, l16_ref))):
        rows = tm // d
        for r in range(d):
            ln_ref[pi, pl.ds(r, rows, stride=d), :] = l_ref[:, r * LANES:(r + 1) * LANES]
            for c in range(n_chunks):
                lo = r * ATTN_W + c * LANES
                on_ref[pi, c, pl.ds(r, rows, stride=d), :] = o_ref[:, lo:lo + LANES].astype(F32)

    la, lb, lc = l1_ref[...], ln_ref[0], ln_ref[1]
    lmax = jnp.maximum(jnp.maximum(la, lb), lc)
    wa, wb, wc = jnp.exp(la - lmax), jnp.exp(lb - lmax), jnp.exp(lc - lmax)
    inv = 1.0 / (wa + wb + wc)
    wa, wb, wc = wa * inv, wb * inv, wc * inv
    first_head = lax.broadcasted_iota(jnp.int32, (tm, LANES), 1) < HEAD_DIM
    per_chunk = LANES // HEAD_DIM
    for c in range(n_chunks):
        h0 = c * per_chunk
        spread = lambda w: jnp.where(first_head, w[:, h0:h0 + 1], w[:, h0 + 1:h0 + 2])
        lo = c * LANES
        attn = (spread(wa) * o1_ref[:, lo:lo + LANES].astype(F32)
                + spread(wb) * on_ref[0, c] + spread(wc) * on_ref[1, c])
        cat_ref[:, lo:lo + LANES] = attn.astype(BF16)

    mix = jnp.dot(cat_ref[...], wo_ref[...], preferred_element_type=F32)
    gate1 = mod_ref[2:3, :]
    xn = x_ref[...] + gate1 * mix
    xn_ref[...] = xn
    h2 = _rmsnorm_rows(xn, g2_ref[...]) * (1.0 + mod_ref[4:5, :]) + mod_ref[3:4, :]

    hi = h2.astype(BF16)
    hi_f = hi.astype(F32)
    h2_ref[...] = _pack_halves(hi_f)
    lo = (h2 - hi_f).astype(BF16)
    wr = wr_ref[...]
    both = jnp.dot(hi, wr, preferred_element_type=F32) + jnp.dot(lo, wr, preferred_element_type=F32)
    bt = both.T
    logit = bt[0:N_EXPERTS, :] + bt[N_EXPERTS:2 * N_EXPERTS, :] + br_ref[...]

    row = lax.broadcasted_iota(jnp.int32, (N_EXPERTS, tm), 0).astype(F32)
    work = logit
    sel = jnp.zeros((N_EXPERTS, tm), F32)
    idxs, vals, hots = [], [], []
    for _ in range(TOP_K):
        m = jnp.max(work, axis=0, keepdims=True)
        idx = jnp.min(jnp.where(work == m, row, float(N_EXPERTS)), axis=0, keepdims=True)
        hot = row == idx
        work = jnp.where(hot, -jnp.inf, work)
        sel = jnp.where(hot, 1.0, sel)
        idxs.append(idx.astype(jnp.int32))
        vals.append(m)
        hots.append(hot)
    exps = [jnp.exp(v - vals[0]) for v in vals]
    esum = exps[0] + exps[1] + exps[2] + exps[3]
    gates = [e / esum for e in exps]

    tr = lax.broadcasted_iota(jnp.int32, (tm, tm), 0)
    tc = lax.broadcasted_iota(jnp.int32, (tm, tm), 1)
    before = jnp.where(tr < tc, 1.0, 0.0).astype(BF16)
    cnt = jnp.dot(sel.astype(BF16), before, preferred_element_type=F32) + carry_ref[:, 0:1]
    ranks = [jnp.sum(jnp.where(h, cnt, 0.0), axis=0, keepdims=True).astype(jnp.int32) for h in hots]
    carry_ref[...] = carry_ref[...] + jnp.sum(sel, axis=1, keepdims=True)
    cnt_ref[...] = carry_ref[...]

    row8 = lax.broadcasted_iota(jnp.int32, (8, tm), 0)
    e8 = jnp.zeros((8, tm), jnp.int32)
    r8 = jnp.zeros((8, tm), jnp.int32)
    for k in range(TOP_K):
        e8 = jnp.where(row8 == k, idxs[k], e8)
        r8 = jnp.where(row8 == k, ranks[k], r8)
    e_ref[...] = e8
    r_ref[...] = r8
    rowg = lax.broadcasted_iota(jnp.int32, (LANES, tm), 0)
    g_t = jnp.zeros((LANES, tm), F32)
    for k in range(TOP_K):
        g_t = jnp.where(rowg == k, gates[k], g_t)
    gt_ref[...] = g_t.T


def _post(x2, mod8, mod_off, attn_outs, z, conv_w, conv_b, ln_g, ln_b, w_out, g2, w_router2, b_router_t,
          cnt0, batch, seq):
    tm = POST_TM
    tokens = batch * seq
    tiles_per_row = seq // tm
    nh = tm // CONV_HALO
    last_h = tokens // CONV_HALO - 1
    row = lambda w: pl.BlockSpec((tm, w), lambda i: (i, 0))
    full = lambda a, b: pl.BlockSpec((a, b), lambda i: (0, 0))
    (o1, l1), (o4, l4), (o16, l16) = attn_outs
    d4, d16 = DILATIONS[1], DILATIONS[2]
    strided = lambda d, w: pl.BlockSpec((tm // d, d * w), lambda i: (i, 0))
    out_shapes = (jax.ShapeDtypeStruct((tokens, D_MODEL), F32),
                  jax.ShapeDtypeStruct((tokens, ROW_WORDS), jnp.uint32),
                  jax.ShapeDtypeStruct((8, tokens), jnp.int32),
                  jax.ShapeDtypeStruct((8, tokens), jnp.int32),
                  jax.ShapeDtypeStruct((tokens, LANES), F32),
                  jax.ShapeDtypeStruct((N_EXPERTS, LANES), F32))
    return pl.pallas_call(
        functools.partial(_post_kernel, tiles_per_row=tiles_per_row),
        out_shape=out_shapes,
        grid=(tokens // tm,),
        in_specs=[row(D_MODEL),
                  pl.BlockSpec((None, 8, D_MODEL), lambda i: (mod_off + i // tiles_per_row, 0, 0)),
                  row(ATTN_W), strided(d4, ATTN_W), strided(d16, ATTN_W),
                  row(LANES), strided(d4, LANES), strided(d16, LANES),
                  pl.BlockSpec((CONV_HALO, CONV_W), lambda i: (jnp.maximum(i * nh - 1, 0), 0)),
                  row(CONV_W),
                  pl.BlockSpec((CONV_HALO, CONV_W), lambda i: (jnp.minimum((i + 1) * nh, last_h), 0)),
                  full(32, CONV_W), full(1, CONV_W), full(1, CONV_W), full(1, CONV_W),
                  full(D_MODEL, D_MODEL), full(1, D_MODEL), full(D_MODEL, ROUTER_W),
                  full(N_EXPERTS, tm), full(N_EXPERTS, LANES)],
        out_specs=(row(D_MODEL), row(ROW_WORDS),
                   pl.BlockSpec((8, tm), lambda i: (0, i)), pl.BlockSpec((8, tm), lambda i: (0, i)),
                   row(LANES), full(N_EXPERTS, LANES)),
        scratch_shapes=[pltpu.VMEM((tm + 2 * CONV_HALO, CONV_W), F32),
                        pltpu.VMEM((SUBLANES, tm + 2 * CONV_HALO, CONV_W), F32),
                        pltpu.VMEM((tm, ATTN_W + CONV_W), BF16),
                        pltpu.VMEM((N_EXPERTS, LANES), F32),
                        pltpu.VMEM((2, ATTN_W // LANES, tm, LANES), F32),
                        pltpu.VMEM((2, tm, LANES), F32)],
        compiler_params=_cparams("arbitrary"),
        name="post",
    )(x2, mod8, o1, o4, o16, l1, l4, l16, z, z, z, conv_w, conv_b, ln_g, ln_b, w_out, g2,
      w_router2, b_router_t, cnt0)


MOE_BM = 256

SLOT_TL = 2048
SC_CORES = 2
SC_SUBCORES = 16
SC_WORKERS = SC_CORES * SC_SUBCORES
SC_CHUNK = 64


def _slot_kernel(plan_ref, e_ref, r_ref, s_ref):
    e = e_ref[...]
    start = jnp.zeros(e.shape, jnp.int32)
    for x in range(N_EXPERTS):
        start = jnp.where(e == x, plan_ref[x], start)
    s_ref[...] = start + r_ref[...]


def _slots(plan, e8, r8):
    tokens = e8.shape[1]
    spec = pl.BlockSpec((8, SLOT_TL), lambda i, ps: (0, i))
    s8 = pl.pallas_call(
        _slot_kernel,
        out_shape=jax.ShapeDtypeStruct((8, tokens), jnp.int32),
        grid_spec=pltpu.PrefetchScalarGridSpec(
            num_scalar_prefetch=1, grid=(tokens // SLOT_TL,), in_specs=[spec, spec], out_specs=spec),
        compiler_params=_cparams("arbitrary"),
        name="slots",
    )(plan, e8, r8)
    return s8[:TOP_K].reshape(TOP_K, tokens // SC_CHUNK, SC_CHUNK).transpose(1, 0, 2)


def _sc_worker_chunks(tokens):
    return tokens // (SC_WORKERS * SC_CHUNK)


def _sc_mesh():
    return plsc.VectorSubcoreMesh(core_axis_name="c", subcore_axis_name="s")


def _dispatch(slots, h2, n_pad):
    tokens = h2.shape[0]
    n_chunks = _sc_worker_chunks(tokens)

    @functools.partial(
        pl.kernel, mesh=_sc_mesh(),
        out_type=jax.ShapeDtypeStruct((n_pad, ROW_WORDS), jnp.uint32),
        scratch_types=[pltpu.VMEM((TOP_K, SC_CHUNK), jnp.int32),
                       pltpu.VMEM((SC_CHUNK, ROW_WORDS), jnp.uint32),
                       pltpu.SemaphoreType.DMA],
        name="sc_dispatch")
    def scatter(h_hbm, slot_hbm, rows_hbm, idx_v, rows_v, sem):
        worker = lax.axis_index("s") * SC_CORES + lax.axis_index("c")

        @pl.loop(0, n_chunks)
        def _(ci):
            chunk = worker * n_chunks + ci
            pltpu.sync_copy(slot_hbm.at[chunk], idx_v)
            pltpu.sync_copy(h_hbm.at[pl.ds(chunk * SC_CHUNK, SC_CHUNK)], rows_v)
            copies = [pltpu.async_copy(rows_v, rows_hbm.at[idx_v.at[k]], sem) for k in range(TOP_K)]
            for cp in copies:
                cp.wait()

    return scatter(h2, slots)


FF_CHUNK = 512


def _experts_kernel(bexp_ref, used_ref, x_ref, wu_ref, bu_ref, wd_ref, bd_ref, y_ref):
    del bexp_ref

    @pl.when(pl.program_id(0) < used_ref[0])
    def _():
        x_lo, x_hi = _unpack_halves(x_ref[...])
        xb = jnp.concatenate([x_lo.astype(BF16), x_hi.astype(BF16)], axis=1)
        acc = jnp.broadcast_to(bd_ref[...], (MOE_BM, D_MODEL))
        for c in range(D_FF // FF_CHUNK):
            lo = c * FF_CHUNK
            glu = jnp.dot(xb, wu_ref[:, lo:lo + FF_CHUNK], preferred_element_type=F32) + bu_ref[:, lo:lo + FF_CHUNK]
            lin = (jnp.dot(xb, wu_ref[:, D_FF + lo:D_FF + lo + FF_CHUNK], preferred_element_type=F32)
                   + bu_ref[:, D_FF + lo:D_FF + lo + FF_CHUNK])
            glu = jnp.minimum(glu, SWIGLU_LIMIT)
            lin = jnp.clip(lin, -SWIGLU_LIMIT, SWIGLU_LIMIT)
            act = glu * jax.nn.sigmoid(SWIGLU_ALPHA * glu) * (lin + 1.0)
            acc = acc + jnp.dot(act.astype(BF16), wd_ref[lo:lo + FF_CHUNK, :], preferred_element_type=F32)
        y_ref[...] = _pack_halves(acc.astype(BF16).astype(F32))

    @pl.when(pl.program_id(0) >= used_ref[0])
    def _():
        y_ref[...] = jnp.zeros((MOE_BM, ROW_WORDS), jnp.uint32)


def _experts(blk_exp, used, rows, w_up, b_up, w_down, b_down):
    n_pad = rows.shape[0]
    n_blocks = n_pad // MOE_BM
    live = lambda i, be, us: jnp.minimum(i, us[0] - 1)
    return pl.pallas_call(
        _experts_kernel,
        out_shape=jax.ShapeDtypeStruct((n_pad, ROW_WORDS), jnp.uint32),
        grid_spec=pltpu.PrefetchScalarGridSpec(
            num_scalar_prefetch=2, grid=(n_blocks,),
            in_specs=[pl.BlockSpec((MOE_BM, ROW_WORDS), lambda i, be, us: (live(i, be, us), 0)),
                      pl.BlockSpec((None, D_MODEL, 2 * D_FF), lambda i, be, us: (be[live(i, be, us)], 0, 0)),
                      pl.BlockSpec((None, 1, 2 * D_FF), lambda i, be, us: (be[live(i, be, us)], 0, 0)),
                      pl.BlockSpec((None, D_FF, D_MODEL), lambda i, be, us: (be[live(i, be, us)], 0, 0)),
                      pl.BlockSpec((None, 1, D_MODEL), lambda i, be, us: (be[live(i, be, us)], 0, 0))],
            out_specs=pl.BlockSpec((MOE_BM, ROW_WORDS), lambda i, be, us: (i, 0))),
        compiler_params=_cparams("arbitrary"),
        name="experts",
    )(blk_exp, used, rows, w_up, b_up, w_down, b_down)


COMBINE_TM = 256


def _gather_rows(slots, y_rows, tokens):
    n_chunks = _sc_worker_chunks(tokens)

    @functools.partial(
        pl.kernel, mesh=_sc_mesh(),
        out_type=jax.ShapeDtypeStruct((TOP_K, tokens, ROW_WORDS), jnp.uint32),
        scratch_types=[pltpu.VMEM((TOP_K, SC_CHUNK), jnp.int32),
                       pltpu.VMEM((SC_CHUNK, ROW_WORDS), jnp.uint32),
                       pltpu.SemaphoreType.DMA],
        name="sc_gather")
    def gather(y_hbm, slot_hbm, out_hbm, idx_v, rows_v, sem):
        worker = lax.axis_index("s") * SC_CORES + lax.axis_index("c")

        @pl.loop(0, n_chunks)
        def _(ci):
            chunk = worker * n_chunks + ci
            pltpu.sync_copy(slot_hbm.at[chunk], idx_v)
            for k in range(TOP_K):
                pltpu.async_copy(y_hbm.at[idx_v.at[k]], rows_v, sem).wait()
                pltpu.sync_copy(rows_v, out_hbm.at[k, pl.ds(chunk * SC_CHUNK, SC_CHUNK)])

    return gather(y_rows, slots)


def _combine_kernel(xn_ref, gt_ref, mod_ref, fg_ref, rows_ref, out_ref):
    gt = gt_ref[...]
    y_lo = y_hi = None
    for k in range(TOP_K):
        lo, hi = _unpack_halves(rows_ref[k])
        g = gt[:, k:k + 1]
        y_lo = g * lo if y_lo is None else y_lo + g * lo
        y_hi = g * hi if y_hi is None else y_hi + g * hi
    y = jnp.concatenate([y_lo, y_hi], axis=1)
    xo = xn_ref[...] + mod_ref[5:6, :] * y
    out_ref[...] = _rmsnorm_rows(xo, fg_ref[...])


def _combine(xn, gt, mod8, mod_off, final_g, gathered, batch, seq):
    tm = COMBINE_TM
    tokens = batch * seq
    tiles_per_row = seq // tm
    return pl.pallas_call(
        _combine_kernel,
        out_shape=jax.ShapeDtypeStruct((tokens, D_MODEL), F32),
        grid=(tokens // tm,),
        in_specs=[pl.BlockSpec((tm, D_MODEL), lambda i: (i, 0)),
                  pl.BlockSpec((tm, LANES), lambda i: (i, 0)),
                  pl.BlockSpec((None, 8, D_MODEL), lambda i: (mod_off + i // tiles_per_row, 0, 0)),
                  pl.BlockSpec((1, D_MODEL), lambda i: (0, 0)),
                  pl.BlockSpec((TOP_K, tm, ROW_WORDS), lambda i: (0, i, 0))],
        out_specs=pl.BlockSpec((tm, D_MODEL), lambda i: (i, 0)),
        compiler_params=_cparams("arbitrary"),
        name="combine",
    )(xn, gt, mod8, final_g, gathered)


def _block_plan(counts, n_blocks):
    pcounts = (counts + MOE_BM - 1) // MOE_BM * MOE_BM
    pends = jnp.cumsum(pcounts)
    pstart = (pends - pcounts).astype(jnp.int32)
    used = (pends[-1:] // MOE_BM).astype(jnp.int32)
    blk_start = jnp.arange(n_blocks, dtype=jnp.int32) * MOE_BM
    blk_exp = jnp.minimum(jnp.sum(pends[None, :] <= blk_start[:, None], axis=1), N_EXPERTS - 1)
    return pstart, blk_exp.astype(jnp.int32), used


def _run_group(x, mod8, mod_off, p):
    batch, seq, _ = x.shape
    tokens = batch * seq
    x2 = x.reshape(tokens, D_MODEL)
    qkv, z = _in_proj(x2, mod8, mod_off, p["g1"], p["w_in"], _rope_tables(seq), batch, seq, tm=512)
    attn_outs = [_attention_pattern(*qkv_d, batch, seq, d) for qkv_d, d in zip(qkv, DILATIONS)]
    cnt0 = jnp.zeros((N_EXPERTS, LANES), F32)
    xn, h2, e8, r8, gt, cnt = _post(x2, mod8, mod_off, attn_outs, z, p["conv_w"], p["conv_b"], p["ln_g"],
                                    p["ln_b"], p["w_out"], p["g2"], p["w_router2"], p["b_router_t"],
                                    cnt0, batch, seq)
    n_pad = tokens * TOP_K + N_EXPERTS * MOE_BM
    plan, blk_exp, used = _block_plan(cnt[:, 0].astype(jnp.int32), n_pad // MOE_BM)
    slots = _slots(plan, e8, r8)
    rows = _dispatch(slots, h2, n_pad)
    y_rows = _experts(blk_exp, used, rows, p["w_up"], p["b_up"], p["w_down"], p["b_down"])
    gathered = _gather_rows(slots, y_rows, tokens)
    out = _combine(xn, gt, mod8, mod_off, p["final_g"], gathered, batch, seq)
    return out.reshape(batch, seq, D_MODEL)


def _prepare(w_ada, b_ada, norm_mix_g, w_in, conv_w, conv_b, conv_ln_g, conv_ln_b, w_out, norm_ffn_g,
             w_router, b_router, w_up, b_up, w_down, b_down, final_g):
    wr = w_router[0]
    wr_hi = wr.astype(BF16)
    wr_lo = (wr - wr_hi.astype(F32)).astype(BF16)
    w_router2 = jnp.concatenate(
        [wr_hi, wr_lo, jnp.zeros((D_MODEL, ROUTER_W - 2 * N_EXPERTS), BF16)], axis=1)
    return dict(
        g1=norm_mix_g[0][None, :], w_in=w_in[0].astype(BF16),
        conv_w=jnp.concatenate([conv_w[0], jnp.zeros((32 - CONV_K, CONV_W), F32)], axis=0),
        conv_b=conv_b[0][None, :], ln_g=conv_ln_g[0][None, :], ln_b=conv_ln_b[0][None, :],
        w_out=w_out[0].astype(BF16), g2=norm_ffn_g[0][None, :],
        w_router2=w_router2,
        b_router_t=jnp.broadcast_to(b_router[0][:, None], (N_EXPERTS, POST_TM)),
        w_up=w_up[0].astype(BF16), b_up=b_up[0][:, None, :],
        w_down=w_down[0].astype(BF16), b_down=b_down[0][:, None, :],
        final_g=final_g[None, :])


def kernel(x_prompt, x_sample, c_prompt, c_sample, w_ada, b_ada, norm_mix_g, w_in, conv_w, conv_b,
           conv_ln_g, conv_ln_b, w_out, norm_ffn_g, w_router, b_router, w_up, b_up, w_down, b_down,
           final_g):
    p = _prepare(w_ada, b_ada, norm_mix_g, w_in, conv_w, conv_b, conv_ln_g, conv_ln_b, w_out,
                 norm_ffn_g, w_router, b_router, w_up, b_up, w_down, b_down, final_g)
    nb_p, nb_s = c_prompt.shape[0], c_sample.shape[0]
    c_rows = 16
    c_all = jnp.concatenate([c_prompt, c_sample, jnp.zeros((c_rows - nb_p - nb_s, D_MODEL), F32)], axis=0)
    mod = _modulation(c_all, w_ada[0].astype(BF16), b_ada[0][None, :])
    mod8 = jnp.concatenate([mod.reshape(c_rows, 6, D_MODEL), jnp.zeros((c_rows, 2, D_MODEL), F32)], axis=1)
    y_prompt = _run_group(x_prompt, mod8, 0, p)
    y_sample = _run_group(x_sample, mod8, nb_p, p)
    return (y_prompt, y_sample)
```

```python
import functools

import jax
import jax.numpy as jnp
from jax import lax
from jax.experimental import pallas as pl
from jax.experimental.pallas import tpu as pltpu
from jax.experimental.pallas import tpu_sc as plsc

D_MODEL = 1024
N_HEADS = 8
HEAD_DIM = 64
ATTN_W = N_HEADS * HEAD_DIM
CONV_W = D_MODEL // 2
IN_PROJ_W = 3 * ATTN_W + 2 * CONV_W
WINDOWS = (128, 512, 2048)
DILATIONS = (1, 4, 16)
ATTN_BLOCK = 128
HALF = 64
ROPE_DIM = HEAD_DIM // 4
ROPE_THETA = 500000.0
NEG_INF = -1e30
CONV_K = 31
CONV_PAD = CONV_K // 2
N_EXPERTS = 32
TOP_K = 4
D_FF = D_MODEL
SWIGLU_ALPHA = 1.702
SWIGLU_LIMIT = 7.0
EPS = 1e-6

LANES = 128
SUBLANES = 8
VMEM_LIMIT = 56 * 1024 * 1024

F32 = jnp.float32
BF16 = jnp.bfloat16


def _cparams(*sem):
    return pltpu.CompilerParams(dimension_semantics=sem, vmem_limit_bytes=VMEM_LIMIT)


def _mod_kernel(c_ref, w_ref, b_ref, o_ref):
    c = c_ref[...]
    a = (c * jax.nn.sigmoid(c)).astype(BF16)
    o_ref[...] = jnp.dot(a, w_ref[...], preferred_element_type=F32) + b_ref[...]


def _modulation(c_all, w_ada, b_ada):
    rows = c_all.shape[0]
    n = w_ada.shape[1]
    tn = 1536
    return pl.pallas_call(
        _mod_kernel,
        out_shape=jax.ShapeDtypeStruct((rows, n), F32),
        grid=(n // tn,),
        in_specs=[pl.BlockSpec((rows, D_MODEL), lambda j: (0, 0)),
                  pl.BlockSpec((D_MODEL, tn), lambda j: (0, j)),
                  pl.BlockSpec((1, tn), lambda j: (0, j))],
        out_specs=pl.BlockSpec((rows, tn), lambda j: (0, j)),
        compiler_params=_cparams("arbitrary"),
        name="modulation",
    )(c_all, w_ada, b_ada)


def _rmsnorm_rows(x, g):
    return x * lax.rsqrt(jnp.mean(x * x, axis=-1, keepdims=True) + EPS) * g


def _pack_halves(x):
    n = x.shape[1] // 2
    lo = lax.bitcast_convert_type(x[:, :n], jnp.uint32)
    hi = lax.bitcast_convert_type(x[:, n:], jnp.uint32)
    return (lo >> 16) | (hi & jnp.uint32(0xFFFF0000))


def _unpack_halves(p):
    lo = lax.bitcast_convert_type(p << 16, F32)
    hi = lax.bitcast_convert_type(p & jnp.uint32(0xFFFF0000), F32)
    return lo, hi


def _inproj_kernel(x_ref, mod_ref, g_ref, w_ref, cos_ref, s1_ref, s2_ref,
                   q1_ref, k1_ref, v1_ref, q4_ref, k4_ref, v4_ref, q16_ref, k16_ref, v16_ref,
                   z_ref, stage_ref):
    tm = x_ref.shape[0]
    x = x_ref[...]
    shift = mod_ref[0:1, :]
    scale = mod_ref[1:2, :]
    h = _rmsnorm_rows(x, g_ref[...]) * (1.0 + scale) + shift
    hb = h.astype(BF16)
    cos = cos_ref[...]
    s1 = s1_ref[...]
    s2 = s2_ref[...]

    def rope(p):
        return p * cos + pltpu.roll(p, LANES - ROPE_DIM // 2, 1) * s1 + pltpu.roll(p, ROPE_DIM // 2, 1) * s2

    n_chunks = ATTN_W // LANES
    plans = ((0, lambda p: rope(p) * (HEAD_DIM ** -0.5), (q1_ref, q4_ref, q16_ref)),
             (ATTN_W, rope, (k1_ref, k4_ref, k16_ref)),
             (2 * ATTN_W, lambda p: p, (v1_ref, v4_ref, v16_ref)))
    for col0, finish, (nat_ref, *strided_refs) in plans:
        proj = jnp.dot(hb, w_ref[:, col0:col0 + ATTN_W], preferred_element_type=F32)
        for c in range(n_chunks):
            lo = c * LANES
            val = finish(proj[:, lo:lo + LANES])
            nat_ref[:, lo:lo + LANES] = val.astype(BF16)
            stage_ref[c] = val
        for d, out_ref in zip(DILATIONS[1:], strided_refs):
            for r in range(d):
                for c in range(n_chunks):
                    lo = r * ATTN_W + c * LANES
                    out_ref[:, lo:lo + LANES] = stage_ref[c, pl.ds(r, tm // d, stride=d), :].astype(BF16)
    a = jnp.dot(hb, w_ref[:, 3 * ATTN_W:3 * ATTN_W + CONV_W], preferred_element_type=F32)
    g = jnp.dot(hb, w_ref[:, 3 * ATTN_W + CONV_W:], preferred_element_type=F32)
    z_ref[...] = (a * jax.nn.sigmoid(g)).astype(BF16)


def _rope_tables(seq):
    half = ROPE_DIM // 2
    inv_freq = ROPE_THETA ** (-jnp.arange(half, dtype=F32) * 2.0 / ROPE_DIM)
    ang = jnp.arange(seq, dtype=F32)[:, None] * inv_freq[None, :]
    cos = jnp.cos(ang)
    sin = jnp.sin(ang)
    ones = jnp.ones((seq, HEAD_DIM - ROPE_DIM), F32)
    zeros = jnp.zeros((seq, HEAD_DIM - ROPE_DIM), F32)
    zh = jnp.zeros((seq, half), F32)
    cos_h = jnp.concatenate([cos, cos, ones], axis=1)
    s1_h = jnp.concatenate([-sin, zh, zeros], axis=1)
    s2_h = jnp.concatenate([zh, sin, zeros], axis=1)
    rep = LANES // HEAD_DIM
    return tuple(jnp.tile(t, (1, rep)) for t in (cos_h, s1_h, s2_h))


def _in_proj(x2, mod8, mod_off, g1, w_in, tables, batch, seq, tm):
    tokens = batch * seq
    tiles_per_row = seq // tm
    row_spec = lambda w: pl.BlockSpec((tm, w), lambda i: (i, 0))
    tab_spec = pl.BlockSpec((tm, LANES), lambda i: (i % tiles_per_row, 0))
    qkv_shapes, qkv_specs = [], []
    for d in DILATIONS:
        qkv_shapes += [jax.ShapeDtypeStruct((tokens // d, d * ATTN_W), BF16)] * 3
        qkv_specs += [pl.BlockSpec((tm // d, d * ATTN_W), lambda i: (i, 0))] * 3
    outs = pl.pallas_call(
        _inproj_kernel,
        out_shape=(*qkv_shapes, jax.ShapeDtypeStruct((tokens, CONV_W), BF16)),
        grid=(tokens // tm,),
        in_specs=[row_spec(D_MODEL),
                  pl.BlockSpec((None, 8, D_MODEL), lambda i: (mod_off + i // tiles_per_row, 0, 0)),
                  pl.BlockSpec((1, D_MODEL), lambda i: (0, 0)),
                  pl.BlockSpec((D_MODEL, IN_PROJ_W), lambda i: (0, 0)),
                  tab_spec, tab_spec, tab_spec],
        out_specs=(*qkv_specs, row_spec(CONV_W)),
        scratch_shapes=[pltpu.VMEM((ATTN_W // LANES, tm, LANES), F32)],
        compiler_params=_cparams("arbitrary"),
        name="in_proj",
    )(x2, mod8, g1, w_in, *tables)
    qkv = [outs[3 * n:3 * n + 3] for n in range(len(DILATIONS))]
    return qkv, outs[-1]


def _attn_kernel(q_ref, kp_ref, kc_ref, kn_ref, vp_ref, vc_ref, vn_ref, o_ref, lse_ref,
                 kw_ref, vw_ref, *, tq, length, nres):
    j = pl.program_id(2)
    kw_ref[0:HALF, :] = kp_ref[...]
    kw_ref[HALF:HALF + tq, :] = kc_ref[...]
    kw_ref[HALF + tq:, :] = kn_ref[...]
    vw_ref[0:HALF, :] = vp_ref[...]
    vw_ref[HALF:HALF + tq, :] = vc_ref[...]
    vw_ref[HALF + tq:, :] = vn_ref[...]

    nk = 2 * ATTN_BLOCK
    t_io = lax.broadcasted_iota(jnp.int32, (ATTN_BLOCK, nk), 0)
    u_io = lax.broadcasted_iota(jnp.int32, (ATTN_BLOCK, nk), 1)
    lane = lax.broadcasted_iota(jnp.int32, (ATTN_BLOCK, LANES), 1)
    first_head = lane < HEAD_DIM
    lane1 = lax.broadcasted_iota(jnp.int32, (1, LANES), 1)
    head_keep = [jnp.where(lane1 < HEAD_DIM, 1.0, 0.0).astype(BF16),
                 jnp.where(lane1 < HEAD_DIM, 0.0, 1.0).astype(BF16)]

    for blk in range(tq // ATTN_BLOCK):
        r0 = blk * ATTN_BLOCK
        base = j * tq + r0
        u_min = jnp.maximum(t_io, HALF - base)
        u_max = jnp.minimum(t_io + 2 * HALF, length + HALF - 1 - base)
        mask = (u_io >= u_min) & (u_io <= u_max)
        for res in range(nres):
            lse_tile = jnp.zeros((ATTN_BLOCK, LANES), F32)
            for c in range(ATTN_W // LANES):
                lo = res * ATTN_W + c * LANES
                qc = q_ref[r0:r0 + ATTN_BLOCK, lo:lo + LANES]
                kc = kw_ref[r0:r0 + nk, lo:lo + LANES]
                vc = vw_ref[r0:r0 + nk, lo:lo + LANES]
                outs = []
                for hh in range(LANES // HEAD_DIM):
                    qm = qc * head_keep[hh]
                    s = lax.dot_general(qm, kc, (((1,), (1,)), ((), ())), preferred_element_type=F32)
                    s = jnp.where(mask, s, NEG_INF)
                    m = jnp.max(s, axis=-1, keepdims=True)
                    p = jnp.exp(s - m)
                    den = jnp.sum(p, axis=-1, keepdims=True)
                    pv = jnp.dot(p.astype(BF16), vc, preferred_element_type=F32)
                    outs.append(pv / den)
                    lse_tile = jnp.where(lane == c * (LANES // HEAD_DIM) + hh, m + jnp.log(den), lse_tile)
                o_ref[r0:r0 + ATTN_BLOCK, lo:lo + LANES] = jnp.where(first_head, outs[0], outs[1]).astype(BF16)
            lse_ref[r0:r0 + ATTN_BLOCK, res * LANES:(res + 1) * LANES] = lse_tile


ATTN_QUERIES = 512


def _attention_pattern(q, k, v, batch, seq, d):
    length = seq // d
    tq = min(length, ATTN_QUERIES)
    nres = min(d, ATTN_QUERIES // tq)
    view = lambda t: t.reshape(batch, length, d * ATTN_W)
    nh = tq // HALF
    last_h = length // HALF - 1
    width = nres * ATTN_W
    cur = pl.BlockSpec((None, tq, width), lambda b, r, j: (b, j, r))
    prev = pl.BlockSpec((None, HALF, width), lambda b, r, j: (b, jnp.maximum(j * nh - 1, 0), r))
    nxt = pl.BlockSpec((None, HALF, width), lambda b, r, j: (b, jnp.minimum((j + 1) * nh, last_h), r))
    o, lse = pl.pallas_call(
        functools.partial(_attn_kernel, tq=tq, length=length, nres=nres),
        out_shape=(jax.ShapeDtypeStruct((batch, length, d * ATTN_W), BF16),
                   jax.ShapeDtypeStruct((batch, length, d * LANES), F32)),
        grid=(batch, d // nres, length // tq),
        in_specs=[cur, prev, cur, nxt, prev, cur, nxt],
        out_specs=(cur, pl.BlockSpec((None, tq, nres * LANES), lambda b, r, j: (b, j, r))),
        scratch_shapes=[pltpu.VMEM((tq + 2 * HALF, width), BF16),
                        pltpu.VMEM((tq + 2 * HALF, width), BF16)],
        compiler_params=_cparams("arbitrary", "arbitrary", "arbitrary"),
        name=f"attention_d{d}",
    )(view(q), view(k), view(k), view(k), view(v), view(v), view(v))
    rows = batch * length
    return o.reshape(rows, d * ATTN_W), lse.reshape(rows, d * LANES)


POST_TM = 256
CONV_HALO = 16
CONV_ROWS = 64
ROUTER_W = 128
ROW_WORDS = D_MODEL // 2


def _post_kernel(x_ref, mod_ref, o1_ref, o4_ref, o16_ref, l1_ref, l4_ref, l16_ref,
                 zp_ref, zc_ref, zn_ref, cw_ref, cb_ref, lng_ref, lnb_ref,
                 wo_ref, g2_ref, wr_ref, br_ref, cnt0_ref,
                 xn_ref, h2_ref, e_ref, r_ref, gt_ref, cnt_ref,
                 zw_ref, zs_ref, cat_ref, carry_ref, on_ref, ln_ref, *, tiles_per_row):
    tm = POST_TM
    i = pl.program_id(0)

    @pl.when(i == 0)
    def _():
        carry_ref[...] = cnt0_ref[...]

    pos = i % tiles_per_row
    zp = zp_ref[...].astype(F32)
    zn = zn_ref[...].astype(F32)
    zw_ref[0:CONV_HALO, :] = jnp.where(pos == 0, 0.0, zp)
    zw_ref[CONV_HALO:CONV_HALO + tm, :] = zc_ref[...].astype(F32)
    zw_ref[CONV_HALO + tm:, :] = jnp.where(pos == tiles_per_row - 1, 0.0, zn)
    cw = cw_ref[...]
    off = CONV_HALO - CONV_PAD
    span = tm + 2 * CONV_HALO - SUBLANES
    for s in range(SUBLANES):
        zs_ref[s, 0:span, :] = zw_ref[s:s + span, :]
    for rc in range(tm // CONV_ROWS):
        r0 = rc * CONV_ROWS
        acc = jnp.broadcast_to(cb_ref[...], (CONV_ROWS, CONV_W))
        for tap in range(CONV_K):
            q, s = divmod(off + tap, SUBLANES)
            lo = r0 + q * SUBLANES
            acc = acc + zs_ref[s, lo:lo + CONV_ROWS, :] * cw[tap:tap + 1, :]
        mu = jnp.mean(acc, axis=-1, keepdims=True)
        cen = acc - mu
        var = jnp.mean(cen * cen, axis=-1, keepdims=True)
        zf = cen * lax.rsqrt(var + EPS) * lng_ref[...] + lnb_ref[...]
        cat_ref[r0:r0 + CONV_ROWS, ATTN_W:] = (zf * jax.nn.sigmoid(zf)).astype(BF16)

    n_chunks = ATTN_W // LANES
    for pi, (d, o_ref, l_ref) in enumerate(((DILATIONS[1], o4_ref, l4_ref), (DILATIONS[2], o16_ref, l16_ref))):
        rows = tm // d
        for r in range(d):
            ln_ref[pi, pl.ds(r, rows, stride=d), :] = l_ref[:, r * LANES:(r + 1) * LANES]
            for c in range(n_chunks):
                lo = r * ATTN_W + c * LANES
                on_ref[pi, c, pl.ds(r, rows, stride=d), :] = o_ref[:, lo:lo + LANES].astype(F32)

    la, lb, lc = l1_ref[...], ln_ref[0], ln_ref[1]
    lmax = jnp.maximum(jnp.maximum(la, lb), lc)
    wa, wb, wc = jnp.exp(la - lmax), jnp.exp(lb - lmax), jnp.exp(lc - lmax)
    inv = 1.0 / (wa + wb + wc)
    wa, wb, wc = wa * inv, wb * inv, wc * inv
    first_head = lax.broadcasted_iota(jnp.int32, (tm, LANES), 1) < HEAD_DIM
    per_chunk = LANES // HEAD_DIM
    for c in range(n_chunks):
        h0 = c * per_chunk
        spread = lambda w: jnp.where(first_head, w[:, h0:h0 + 1], w[:, h0 + 1:h0 + 2])
        lo = c * LANES
        attn = (spread(wa) * o1_ref[:, lo:lo + LANES].astype(F32)
                + spread(wb) * on_ref[0, c] + spread(wc) * on_ref[1, c])
        cat_ref[:, lo:lo + LANES] = attn.astype(BF16)

    mix = jnp.dot(cat_ref[...], wo_ref[...], preferred_element_type=F32)
    gate1 = mod_ref[2:3, :]
    xn = x_ref[...] + gate1 * mix
    xn_ref[...] = xn
    h2 = _rmsnorm_rows(xn, g2_ref[...]) * (1.0 + mod_ref[4:5, :]) + mod_ref[3:4, :]

    hi = h2.astype(BF16)
    hi_f = hi.astype(F32)
    h2_ref[...] = _pack_halves(hi_f)
    lo = (h2 - hi_f).astype(BF16)
    wr = wr_ref[...]
    both = jnp.dot(hi, wr, preferred_element_type=F32) + jnp.dot(lo, wr, preferred_element_type=F32)
    bt = both.T
    logit = bt[0:N_EXPERTS, :] + bt[N_EXPERTS:2 * N_EXPERTS, :] + br_ref[...]

    row = lax.broadcasted_iota(jnp.int32, (N_EXPERTS, tm), 0).astype(F32)
    work = logit
    sel = jnp.zeros((N_EXPERTS, tm), F32)
    idxs, vals, hots = [], [], []
    for _ in range(TOP_K):
        m = jnp.max(work, axis=0, keepdims=True)
        idx = jnp.min(jnp.where(work == m, row, float(N_EXPERTS)), axis=0, keepdims=True)
        hot = row == idx
        work = jnp.where(hot, -jnp.inf, work)
        sel = jnp.where(hot, 1.0, sel)
        idxs.append(idx.astype(jnp.int32))
        vals.append(m)
        hots.append(hot)
    exps = [jnp.exp(v - vals[0]) for v in vals]
    esum = exps[0] + exps[1] + exps[2] + exps[3]
    gates = [e / esum for e in exps]

    tr = lax.broadcasted_iota(jnp.int32, (tm, tm), 0)
    tc = lax.broadcasted_iota(jnp.int32, (tm, tm), 1)
    before = jnp.where(tr < tc, 1.0, 0.0).astype(BF16)
    cnt = jnp.dot(sel.astype(BF16), before, preferred_element_type=F32) + carry_ref[:, 0:1]
    ranks = [jnp.sum(jnp.where(h, cnt, 0.0), axis=0, keepdims=True).astype(jnp.int32) for h in hots]
    carry_ref[...] = carry_ref[...] + jnp.sum(sel, axis=1, keepdims=True)
    cnt_ref[...] = carry_ref[...]

    row8 = lax.broadcasted_iota(jnp.int32, (8, tm), 0)
    e8 = jnp.zeros((8, tm), jnp.int32)
    r8 = jnp.zeros((8, tm), jnp.int32)
    for k in range(TOP_K):
        e8 = jnp.where(row8 == k, idxs[k], e8)
        r8 = jnp.where(row8 == k, ranks[k], r8)
    e_ref[...] = e8
    r_ref[...] = r8
    rowg = lax.broadcasted_iota(jnp.int32, (LANES, tm), 0)
    g_t = jnp.zeros((LANES, tm), F32)
    for k in range(TOP_K):
        g_t = jnp.where(rowg == k, gates[k], g_t)
    gt_ref[...] = g_t.T


def _post(x2, mod8, mod_off, attn_outs, z, conv_w, conv_b, ln_g, ln_b, w_out, g2, w_router2, b_router_t,
          cnt0, batch, seq):
    tm = POST_TM
    tokens = batch * seq
    tiles_per_row = seq // tm
    nh = tm // CONV_HALO
    last_h = tokens // CONV_HALO - 1
    row = lambda w: pl.BlockSpec((tm, w), lambda i: (i, 0))
    full = lambda a, b: pl.BlockSpec((a, b), lambda i: (0, 0))
    (o1, l1), (o4, l4), (o16, l16) = attn_outs
    d4, d16 = DILATIONS[1], DILATIONS[2]
    strided = lambda d, w: pl.BlockSpec((tm // d, d * w), lambda i: (i, 0))
    out_shapes = (jax.ShapeDtypeStruct((tokens, D_MODEL), F32),
                  jax.ShapeDtypeStruct((tokens, ROW_WORDS), jnp.uint32),
                  jax.ShapeDtypeStruct((8, tokens), jnp.int32),
                  jax.ShapeDtypeStruct((8, tokens), jnp.int32),
                  jax.ShapeDtypeStruct((tokens, LANES), F32),
                  jax.ShapeDtypeStruct((N_EXPERTS, LANES), F32))
    return pl.pallas_call(
        functools.partial(_post_kernel, tiles_per_row=tiles_per_row),
        out_shape=out_shapes,
        grid=(tokens // tm,),
        in_specs=[row(D_MODEL),
                  pl.BlockSpec((None, 8, D_MODEL), lambda i: (mod_off + i // tiles_per_row, 0, 0)),
                  row(ATTN_W), strided(d4, ATTN_W), strided(d16, ATTN_W),
                  row(LANES), strided(d4, LANES), strided(d16, LANES),
                  pl.BlockSpec((CONV_HALO, CONV_W), lambda i: (jnp.maximum(i * nh - 1, 0), 0)),
                  row(CONV_W),
                  pl.BlockSpec((CONV_HALO, CONV_W), lambda i: (jnp.minimum((i + 1) * nh, last_h), 0)),
                  full(32, CONV_W), full(1, CONV_W), full(1, CONV_W), full(1, CONV_W),
                  full(D_MODEL, D_MODEL), full(1, D_MODEL), full(D_MODEL, ROUTER_W),
                  full(N_EXPERTS, tm), full(N_EXPERTS, LANES)],
        out_specs=(row(D_MODEL), row(ROW_WORDS),
                   pl.BlockSpec((8, tm), lambda i: (0, i)), pl.BlockSpec((8, tm), lambda i: (0, i)),
                   row(LANES), full(N_EXPERTS, LANES)),
        scratch_shapes=[pltpu.VMEM((tm + 2 * CONV_HALO, CONV_W), F32),
                        pltpu.VMEM((SUBLANES, tm + 2 * CONV_HALO, CONV_W), F32),
                        pltpu.VMEM((tm, ATTN_W + CONV_W), BF16),
                        pltpu.VMEM((N_EXPERTS, LANES), F32),
                        pltpu.VMEM((2, ATTN_W // LANES, tm, LANES), F32),
                        pltpu.VMEM((2, tm, LANES), F32)],
        compiler_params=_cparams("arbitrary"),
        name="post",
    )(x2, mod8, o1, o4, o16, l1, l4, l16, z, z, z, conv_w, conv_b, ln_g, ln_b, w_out, g2,
      w_router2, b_router_t, cnt0)


MOE_BM = 256

SLOT_TL = 2048
SC_CORES = 2
SC_SUBCORES = 16
SC_WORKERS = SC_CORES * SC_SUBCORES
SC_CHUNK = 64


def _slot_kernel(plan_ref, e_ref, r_ref, s_ref):
    e = e_ref[...]
    start = jnp.zeros(e.shape, jnp.int32)
    for x in range(N_EXPERTS):
        start = jnp.where(e == x, plan_ref[x], start)
    s_ref[...] = start + r_ref[...]


def _slots(plan, e8, r8):
    tokens = e8.shape[1]
    spec = pl.BlockSpec((8, SLOT_TL), lambda i, ps: (0, i))
    s8 = pl.pallas_call(
        _slot_kernel,
        out_shape=jax.ShapeDtypeStruct((8, tokens), jnp.int32),
        grid_spec=pltpu.PrefetchScalarGridSpec(
            num_scalar_prefetch=1, grid=(tokens // SLOT_TL,), in_specs=[spec, spec], out_specs=spec),
        compiler_params=_cparams("arbitrary"),
        name="slots",
    )(plan, e8, r8)
    return s8[:TOP_K].reshape(TOP_K, tokens // SC_CHUNK, SC_CHUNK).transpose(1, 0, 2)


def _sc_worker_chunks(tokens):
    return tokens // (SC_WORKERS * SC_CHUNK)


def _sc_mesh():
    return plsc.VectorSubcoreMesh(core_axis_name="c", subcore_axis_name="s")


def _dispatch(slots, h2, n_pad):
    tokens = h2.shape[0]
    n_chunks = _sc_worker_chunks(tokens)

    @functools.partial(
        pl.kernel, mesh=_sc_mesh(),
        out_type=jax.ShapeDtypeStruct((n_pad, ROW_WORDS), jnp.uint32),
        scratch_types=[pltpu.VMEM((TOP_K, SC_CHUNK), jnp.int32),
                       pltpu.VMEM((SC_CHUNK, ROW_WORDS), jnp.uint32),
                       pltpu.SemaphoreType.DMA],
        name="sc_dispatch")
    def scatter(h_hbm, slot_hbm, rows_hbm, idx_v, rows_v, sem):
        worker = lax.axis_index("s") * SC_CORES + lax.axis_index("c")

        @pl.loop(0, n_chunks)
        def _(ci):
            chunk = worker * n_chunks + ci
            pltpu.sync_copy(slot_hbm.at[chunk], idx_v)
            pltpu.sync_copy(h_hbm.at[pl.ds(chunk * SC_CHUNK, SC_CHUNK)], rows_v)
            copies = [pltpu.async_copy(rows_v, rows_hbm.at[idx_v.at[k]], sem) for k in range(TOP_K)]
            for cp in copies:
                cp.wait()

    return scatter(h2, slots)


FF_CHUNK = 512


CAST_ROWS = 64


def _experts_kernel(nblk_ref, bstart_ref, x_hbm, wu_ref, bu_ref, wd_ref, bd_ref, y_hbm,
                    wub_ref, wdb_ref, xbuf_ref, ybuf_ref, sem_in, sem_out):
    e = pl.program_id(0)
    n = nblk_ref[e]
    b0 = bstart_ref[e]

    def rows_at(ref, j):
        return ref.at[pl.ds(pl.multiple_of((b0 + j) * MOE_BM, MOE_BM), MOE_BM)]

    def x_copy(j, slot):
        return pltpu.make_async_copy(rows_at(x_hbm, j), xbuf_ref.at[slot], sem_in.at[slot])

    def y_copy(j, slot):
        return pltpu.make_async_copy(ybuf_ref.at[slot], rows_at(y_hbm, j), sem_out.at[slot])

    @pl.when(n > 0)
    def _():
        x_copy(0, 0).start()

        def cast_rows(i, carry):
            r = pl.multiple_of(i * CAST_ROWS, CAST_ROWS)
            wub_ref[pl.ds(r, CAST_ROWS), :] = wu_ref[pl.ds(r, CAST_ROWS), :].astype(BF16)
            wdb_ref[pl.ds(r, CAST_ROWS), :] = wd_ref[pl.ds(r, CAST_ROWS), :].astype(BF16)
            return carry

        lax.fori_loop(0, D_MODEL // CAST_ROWS, cast_rows, 0)

        def block(j, carry):
            slot = j % 2
            x_copy(j, slot).wait()

            @pl.when(j + 1 < n)
            def _():
                x_copy(j + 1, 1 - slot).start()

            @pl.when(j >= 2)
            def _():
                y_copy(j - 2, slot).wait()

            x_lo, x_hi = _unpack_halves(xbuf_ref[slot])
            xb = jnp.concatenate([x_lo.astype(BF16), x_hi.astype(BF16)], axis=1)
            acc = jnp.broadcast_to(bd_ref[...], (MOE_BM, D_MODEL))
            for c in range(D_FF // FF_CHUNK):
                lo = c * FF_CHUNK
                glu = (jnp.dot(xb, wub_ref[:, lo:lo + FF_CHUNK], preferred_element_type=F32)
                       + bu_ref[:, lo:lo + FF_CHUNK])
                lin = (jnp.dot(xb, wub_ref[:, D_FF + lo:D_FF + lo + FF_CHUNK], preferred_element_type=F32)
                       + bu_ref[:, D_FF + lo:D_FF + lo + FF_CHUNK])
                glu = jnp.minimum(glu, SWIGLU_LIMIT)
                lin = jnp.clip(lin, -SWIGLU_LIMIT, SWIGLU_LIMIT)
                act = glu * jax.nn.sigmoid(SWIGLU_ALPHA * glu) * (lin + 1.0)
                acc = acc + jnp.dot(act.astype(BF16), wdb_ref[lo:lo + FF_CHUNK, :], preferred_element_type=F32)
            ybuf_ref[slot] = _pack_halves(acc.astype(BF16).astype(F32))
            y_copy(j, slot).start()
            return carry

        lax.fori_loop(0, n, block, 0)

        @pl.when(n >= 2)
        def _():
            y_copy(n - 2, n % 2).wait()

        y_copy(n - 1, (n - 1) % 2).wait()


def _experts(nblk, bstart, rows, w_up, b_up, w_down, b_down):
    n_pad = rows.shape[0]
    per_expert = lambda *tail: pl.BlockSpec((None, *tail), lambda e, nb, bs: (e, 0, 0))
    return pl.pallas_call(
        _experts_kernel,
        out_shape=jax.ShapeDtypeStruct((n_pad, ROW_WORDS), jnp.uint32),
        grid_spec=pltpu.PrefetchScalarGridSpec(
            num_scalar_prefetch=2, grid=(N_EXPERTS,),
            in_specs=[pl.BlockSpec(memory_space=pl.ANY),
                      per_expert(D_MODEL, 2 * D_FF), per_expert(1, 2 * D_FF),
                      per_expert(D_FF, D_MODEL), per_expert(1, D_MODEL)],
            out_specs=pl.BlockSpec(memory_space=pl.ANY),
            scratch_shapes=[pltpu.VMEM((D_MODEL, 2 * D_FF), BF16), pltpu.VMEM((D_FF, D_MODEL), BF16),
                            pltpu.VMEM((2, MOE_BM, ROW_WORDS), jnp.uint32),
                            pltpu.VMEM((2, MOE_BM, ROW_WORDS), jnp.uint32),
                            pltpu.SemaphoreType.DMA((2,)), pltpu.SemaphoreType.DMA((2,))]),
        compiler_params=_cparams("arbitrary"),
        name="experts",
    )(nblk, bstart, rows, w_up, b_up, w_down, b_down)


COMBINE_TM = 256


def _gather_rows(slots, y_rows, tokens):
    n_chunks = _sc_worker_chunks(tokens)

    @functools.partial(
        pl.kernel, mesh=_sc_mesh(),
        out_type=jax.ShapeDtypeStruct((TOP_K, tokens, ROW_WORDS), jnp.uint32),
        scratch_types=[pltpu.VMEM((TOP_K, SC_CHUNK), jnp.int32),
                       pltpu.VMEM((SC_CHUNK, ROW_WORDS), jnp.uint32),
                       pltpu.SemaphoreType.DMA],
        name="sc_gather")
    def gather(y_hbm, slot_hbm, out_hbm, idx_v, rows_v, sem):
        worker = lax.axis_index("s") * SC_CORES + lax.axis_index("c")

        @pl.loop(0, n_chunks)
        def _(ci):
            chunk = worker * n_chunks + ci
            pltpu.sync_copy(slot_hbm.at[chunk], idx_v)
            for k in range(TOP_K):
                pltpu.async_copy(y_hbm.at[idx_v.at[k]], rows_v, sem).wait()
                pltpu.sync_copy(rows_v, out_hbm.at[k, pl.ds(chunk * SC_CHUNK, SC_CHUNK)])

    return gather(y_rows, slots)


def _combine_kernel(xn_ref, gt_ref, mod_ref, fg_ref, rows_ref, out_ref):
    gt = gt_ref[...]
    y_lo = y_hi = None
    for k in range(TOP_K):
        lo, hi = _unpack_halves(rows_ref[k])
        g = gt[:, k:k + 1]
        y_lo = g * lo if y_lo is None else y_lo + g * lo
        y_hi = g * hi if y_hi is None else y_hi + g * hi
    y = jnp.concatenate([y_lo, y_hi], axis=1)
    xo = xn_ref[...] + mod_ref[5:6, :] * y
    out_ref[...] = _rmsnorm_rows(xo, fg_ref[...])


def _combine(xn, gt, mod8, mod_off, final_g, gathered, batch, seq):
    tm = COMBINE_TM
    tokens = batch * seq
    tiles_per_row = seq // tm
    return pl.pallas_call(
        _combine_kernel,
        out_shape=jax.ShapeDtypeStruct((tokens, D_MODEL), F32),
        grid=(tokens // tm,),
        in_specs=[pl.BlockSpec((tm, D_MODEL), lambda i: (i, 0)),
                  pl.BlockSpec((tm, LANES), lambda i: (i, 0)),
                  pl.BlockSpec((None, 8, D_MODEL), lambda i: (mod_off + i // tiles_per_row, 0, 0)),
                  pl.BlockSpec((1, D_MODEL), lambda i: (0, 0)),
                  pl.BlockSpec((TOP_K, tm, ROW_WORDS), lambda i: (0, i, 0))],
        out_specs=pl.BlockSpec((tm, D_MODEL), lambda i: (i, 0)),
        compiler_params=_cparams("arbitrary"),
        name="combine",
    )(xn, gt, mod8, final_g, gathered)


def _block_plan(counts):
    nblk = (counts + MOE_BM - 1) // MOE_BM
    bstart = jnp.cumsum(nblk) - nblk
    return (bstart * MOE_BM).astype(jnp.int32), nblk.astype(jnp.int32), bstart.astype(jnp.int32)


def _run_group(x, mod8, mod_off, p):
    batch, seq, _ = x.shape
    tokens = batch * seq
    x2 = x.reshape(tokens, D_MODEL)
    qkv, z = _in_proj(x2, mod8, mod_off, p["g1"], p["w_in"], _rope_tables(seq), batch, seq, tm=512)
    attn_outs = [_attention_pattern(*qkv_d, batch, seq, d) for qkv_d, d in zip(qkv, DILATIONS)]
    cnt0 = jnp.zeros((N_EXPERTS, LANES), F32)
    xn, h2, e8, r8, gt, cnt = _post(x2, mod8, mod_off, attn_outs, z, p["conv_w"], p["conv_b"], p["ln_g"],
                                    p["ln_b"], p["w_out"], p["g2"], p["w_router2"], p["b_router_t"],
                                    cnt0, batch, seq)
    n_pad = tokens * TOP_K + N_EXPERTS * MOE_BM
    pstart, nblk, bstart = _block_plan(cnt[:, 0].astype(jnp.int32))
    slots = _slots(pstart, e8, r8)
    rows = _dispatch(slots, h2, n_pad)
    y_rows = _experts(nblk, bstart, rows, p["w_up"], p["b_up"], p["w_down"], p["b_down"])
    gathered = _gather_rows(slots, y_rows, tokens)
    out = _combine(xn, gt, mod8, mod_off, p["final_g"], gathered, batch, seq)
    return out.reshape(batch, seq, D_MODEL)


def _prepare(w_ada, b_ada, norm_mix_g, w_in, conv_w, conv_b, conv_ln_g, conv_ln_b, w_out, norm_ffn_g,
             w_router, b_router, w_up, b_up, w_down, b_down, final_g):
    wr = w_router[0]
    wr_hi = wr.astype(BF16)
    wr_lo = (wr - wr_hi.astype(F32)).astype(BF16)
    w_router2 = jnp.concatenate(
        [wr_hi, wr_lo, jnp.zeros((D_MODEL, ROUTER_W - 2 * N_EXPERTS), BF16)], axis=1)
    return dict(
        g1=norm_mix_g[0][None, :], w_in=w_in[0].astype(BF16),
        conv_w=jnp.concatenate([conv_w[0], jnp.zeros((32 - CONV_K, CONV_W), F32)], axis=0),
        conv_b=conv_b[0][None, :], ln_g=conv_ln_g[0][None, :], ln_b=conv_ln_b[0][None, :],
        w_out=w_out[0].astype(BF16), g2=norm_ffn_g[0][None, :],
        w_router2=w_router2,
        b_router_t=jnp.broadcast_to(b_router[0][:, None], (N_EXPERTS, POST_TM)),
        w_up=w_up[0], b_up=b_up[0][:, None, :],
        w_down=w_down[0], b_down=b_down[0][:, None, :],
        final_g=final_g[None, :])


def kernel(x_prompt, x_sample, c_prompt, c_sample, w_ada, b_ada, norm_mix_g, w_in, conv_w, conv_b,
           conv_ln_g, conv_ln_b, w_out, norm_ffn_g, w_router, b_router, w_up, b_up, w_down, b_down,
           final_g):
    p = _prepare(w_ada, b_ada, norm_mix_g, w_in, conv_w, conv_b, conv_ln_g, conv_ln_b, w_out,
                 norm_ffn_g, w_router, b_router, w_up, b_up, w_down, b_down, final_g)
    nb_p, nb_s = c_prompt.shape[0], c_sample.shape[0]
    c_rows = 16
    c_all = jnp.concatenate([c_prompt, c_sample, jnp.zeros((c_rows - nb_p - nb_s, D_MODEL), F32)], axis=0)
    mod = _modulation(c_all, w_ada[0].astype(BF16), b_ada[0][None, :])
    mod8 = jnp.concatenate([mod.reshape(c_rows, 6, D_MODEL), jnp.zeros((c_rows, 2, D_MODEL), F32)], axis=1)
    y_prompt = _run_group(x_prompt, mod8, 0, p)
    y_sample = _run_group(x_sample, mod8, nb_p, p)
    return (y_prompt, y_sample)
```

```python
import functools

import jax
import jax.numpy as jnp
from jax import lax
from jax.experimental import pallas as pl
from jax.experimental.pallas import tpu as pltpu
from jax.experimental.pallas import tpu_sc as plsc

D_MODEL = 1024
N_HEADS = 8
HEAD_DIM = 64
ATTN_W = N_HEADS * HEAD_DIM
CONV_W = D_MODEL // 2
IN_PROJ_W = 3 * ATTN_W + 2 * CONV_W
WINDOWS = (128, 512, 2048)
DILATIONS = (1, 4, 16)
ATTN_BLOCK = 128
HALF = 64
ROPE_DIM = HEAD_DIM // 4
ROPE_THETA = 500000.0
NEG_INF = -1e30
CONV_K = 31
CONV_PAD = CONV_K // 2
N_EXPERTS = 32
TOP_K = 4
D_FF = D_MODEL
SWIGLU_ALPHA = 1.702
SWIGLU_LIMIT = 7.0
EPS = 1e-6

LANES = 128
SUBLANES = 8
VMEM_LIMIT = 56 * 1024 * 1024

F32 = jnp.float32
BF16 = jnp.bfloat16


def _cparams(*sem):
    return pltpu.CompilerParams(dimension_semantics=sem, vmem_limit_bytes=VMEM_LIMIT)


def _mod_kernel(c_ref, w_ref, b_ref, o_ref):
    c = c_ref[...]
    a = (c * jax.nn.sigmoid(c)).astype(BF16)
    o_ref[...] = jnp.dot(a, w_ref[...], preferred_element_type=F32) + b_ref[...]


def _modulation(c_all, w_ada, b_ada):
    rows = c_all.shape[0]
    n = w_ada.shape[1]
    tn = 1536
    return pl.pallas_call(
        _mod_kernel,
        out_shape=jax.ShapeDtypeStruct((rows, n), F32),
        grid=(n // tn,),
        in_specs=[pl.BlockSpec((rows, D_MODEL), lambda j: (0, 0)),
                  pl.BlockSpec((D_MODEL, tn), lambda j: (0, j)),
                  pl.BlockSpec((1, tn), lambda j: (0, j))],
        out_specs=pl.BlockSpec((rows, tn), lambda j: (0, j)),
        compiler_params=_cparams("arbitrary"),
        name="modulation",
    )(c_all, w_ada, b_ada)


def _rmsnorm_rows(x, g):
    return x * lax.rsqrt(jnp.mean(x * x, axis=-1, keepdims=True) + EPS) * g


def _pack_halves(x):
    n = x.shape[1] // 2
    lo = lax.bitcast_convert_type(x[:, :n], jnp.uint32)
    hi = lax.bitcast_convert_type(x[:, n:], jnp.uint32)
    return (lo >> 16) | (hi & jnp.uint32(0xFFFF0000))


def _unpack_halves(p):
    lo = lax.bitcast_convert_type(p << 16, F32)
    hi = lax.bitcast_convert_type(p & jnp.uint32(0xFFFF0000), F32)
    return lo, hi


def _inproj_kernel(x_ref, mod_ref, g_ref, w_ref, cos_ref, s1_ref, s2_ref,
                   q1_ref, k1_ref, v1_ref, q4_ref, k4_ref, v4_ref, q16_ref, k16_ref, v16_ref,
                   z_ref, stage_ref):
    tm = x_ref.shape[0]
    x = x_ref[...]
    shift = mod_ref[0:1, :]
    scale = mod_ref[1:2, :]
    h = _rmsnorm_rows(x, g_ref[...]) * (1.0 + scale) + shift
    hb = h.astype(BF16)
    cos = cos_ref[...]
    s1 = s1_ref[...]
    s2 = s2_ref[...]

    def rope(p):
        return p * cos + pltpu.roll(p, LANES - ROPE_DIM // 2, 1) * s1 + pltpu.roll(p, ROPE_DIM // 2, 1) * s2

    n_chunks = ATTN_W // LANES
    plans = ((0, lambda p: rope(p) * (HEAD_DIM ** -0.5), (q1_ref, q4_ref, q16_ref)),
             (ATTN_W, rope, (k1_ref, k4_ref, k16_ref)),
             (2 * ATTN_W, lambda p: p, (v1_ref, v4_ref, v16_ref)))
    for col0, finish, (nat_ref, *strided_refs) in plans:
        proj = jnp.dot(hb, w_ref[:, col0:col0 + ATTN_W], preferred_element_type=F32)
        for c in range(n_chunks):
            lo = c * LANES
            val = finish(proj[:, lo:lo + LANES])
            nat_ref[:, lo:lo + LANES] = val.astype(BF16)
            stage_ref[c] = val
        for d, out_ref in zip(DILATIONS[1:], strided_refs):
            for r in range(d):
                for c in range(n_chunks):
                    lo = r * ATTN_W + c * LANES
                    out_ref[:, lo:lo + LANES] = stage_ref[c, pl.ds(r, tm // d, stride=d), :].astype(BF16)
    a = jnp.dot(hb, w_ref[:, 3 * ATTN_W:3 * ATTN_W + CONV_W], preferred_element_type=F32)
    g = jnp.dot(hb, w_ref[:, 3 * ATTN_W + CONV_W:], preferred_element_type=F32)
    z_ref[...] = (a * jax.nn.sigmoid(g)).astype(BF16)


def _rope_tables(seq):
    half = ROPE_DIM // 2
    inv_freq = ROPE_THETA ** (-jnp.arange(half, dtype=F32) * 2.0 / ROPE_DIM)
    ang = jnp.arange(seq, dtype=F32)[:, None] * inv_freq[None, :]
    cos = jnp.cos(ang)
    sin = jnp.sin(ang)
    ones = jnp.ones((seq, HEAD_DIM - ROPE_DIM), F32)
    zeros = jnp.zeros((seq, HEAD_DIM - ROPE_DIM), F32)
    zh = jnp.zeros((seq, half), F32)
    cos_h = jnp.concatenate([cos, cos, ones], axis=1)
    s1_h = jnp.concatenate([-sin, zh, zeros], axis=1)
    s2_h = jnp.concatenate([zh, sin, zeros], axis=1)
    rep = LANES // HEAD_DIM
    return tuple(jnp.tile(t, (1, rep)) for t in (cos_h, s1_h, s2_h))


def _in_proj(x2, mod8, mod_off, g1, w_in, tables, batch, seq, tm):
    tokens = batch * seq
    tiles_per_row = seq // tm
    row_spec = lambda w: pl.BlockSpec((tm, w), lambda i: (i, 0))
    tab_spec = pl.BlockSpec((tm, LANES), lambda i: (i % tiles_per_row, 0))
    qkv_shapes, qkv_specs = [], []
    for d in DILATIONS:
        qkv_shapes += [jax.ShapeDtypeStruct((tokens // d, d * ATTN_W), BF16)] * 3
        qkv_specs += [pl.BlockSpec((tm // d, d * ATTN_W), lambda i: (i, 0))] * 3
    outs = pl.pallas_call(
        _inproj_kernel,
        out_shape=(*qkv_shapes, jax.ShapeDtypeStruct((tokens, CONV_W), BF16)),
        grid=(tokens // tm,),
        in_specs=[row_spec(D_MODEL),
                  pl.BlockSpec((None, 8, D_MODEL), lambda i: (mod_off + i // tiles_per_row, 0, 0)),
                  pl.BlockSpec((1, D_MODEL), lambda i: (0, 0)),
                  pl.BlockSpec((D_MODEL, IN_PROJ_W), lambda i: (0, 0)),
                  tab_spec, tab_spec, tab_spec],
        out_specs=(*qkv_specs, row_spec(CONV_W)),
        scratch_shapes=[pltpu.VMEM((ATTN_W // LANES, tm, LANES), F32)],
        compiler_params=_cparams("arbitrary"),
        name="in_proj",
    )(x2, mod8, g1, w_in, *tables)
    qkv = [outs[3 * n:3 * n + 3] for n in range(len(DILATIONS))]
    return qkv, outs[-1]


def _attn_kernel(q_ref, kp_ref, kc_ref, kn_ref, vp_ref, vc_ref, vn_ref, o_ref, lse_ref,
                 kw_ref, vw_ref, *, tq, length, nres):
    j = pl.program_id(2)
    kw_ref[0:HALF, :] = kp_ref[...]
    kw_ref[HALF:HALF + tq, :] = kc_ref[...]
    kw_ref[HALF + tq:, :] = kn_ref[...]
    vw_ref[0:HALF, :] = vp_ref[...]
    vw_ref[HALF:HALF + tq, :] = vc_ref[...]
    vw_ref[HALF + tq:, :] = vn_ref[...]

    nk = 2 * ATTN_BLOCK
    t_io = lax.broadcasted_iota(jnp.int32, (ATTN_BLOCK, nk), 0)
    u_io = lax.broadcasted_iota(jnp.int32, (ATTN_BLOCK, nk), 1)
    lane = lax.broadcasted_iota(jnp.int32, (ATTN_BLOCK, LANES), 1)
    first_head = lane < HEAD_DIM
    lane1 = lax.broadcasted_iota(jnp.int32, (1, LANES), 1)
    head_keep = [jnp.where(lane1 < HEAD_DIM, 1.0, 0.0).astype(BF16),
                 jnp.where(lane1 < HEAD_DIM, 0.0, 1.0).astype(BF16)]

    for blk in range(tq // ATTN_BLOCK):
        r0 = blk * ATTN_BLOCK
        base = j * tq + r0
        u_min = jnp.maximum(t_io, HALF - base)
        u_max = jnp.minimum(t_io + 2 * HALF, length + HALF - 1 - base)
        mask = (u_io >= u_min) & (u_io <= u_max)
        for res in range(nres):
            lse_tile = jnp.zeros((ATTN_BLOCK, LANES), F32)
            for c in range(ATTN_W // LANES):
                lo = res * ATTN_W + c * LANES
                qc = q_ref[r0:r0 + ATTN_BLOCK, lo:lo + LANES]
                kc = kw_ref[r0:r0 + nk, lo:lo + LANES]
                vc = vw_ref[r0:r0 + nk, lo:lo + LANES]
                outs = []
                for hh in range(LANES // HEAD_DIM):
                    qm = qc * head_keep[hh]
                    s = lax.dot_general(qm, kc, (((1,), (1,)), ((), ())), preferred_element_type=F32)
                    s = jnp.where(mask, s, NEG_INF)
                    m = jnp.max(s, axis=-1, keepdims=True)
                    p = jnp.exp(s - m)
                    den = jnp.sum(p, axis=-1, keepdims=True)
                    pv = jnp.dot(p.astype(BF16), vc, preferred_element_type=F32)
                    outs.append(pv / den)
                    lse_tile = jnp.where(lane == c * (LANES // HEAD_DIM) + hh, m + jnp.log(den), lse_tile)
                o_ref[r0:r0 + ATTN_BLOCK, lo:lo + LANES] = jnp.where(first_head, outs[0], outs[1]).astype(BF16)
            lse_ref[r0:r0 + ATTN_BLOCK, res * LANES:(res + 1) * LANES] = lse_tile


ATTN_QUERIES = 512


def _attention_pattern(q, k, v, batch, seq, d):
    length = seq // d
    tq = min(length, ATTN_QUERIES)
    nres = min(d, ATTN_QUERIES // tq)
    view = lambda t: t.reshape(batch, length, d * ATTN_W)
    nh = tq // HALF
    last_h = length // HALF - 1
    width = nres * ATTN_W
    cur = pl.BlockSpec((None, tq, width), lambda b, r, j: (b, j, r))
    prev = pl.BlockSpec((None, HALF, width), lambda b, r, j: (b, jnp.maximum(j * nh - 1, 0), r))
    nxt = pl.BlockSpec((None, HALF, width), lambda b, r, j: (b, jnp.minimum((j + 1) * nh, last_h), r))
    o, lse = pl.pallas_call(
        functools.partial(_attn_kernel, tq=tq, length=length, nres=nres),
        out_shape=(jax.ShapeDtypeStruct((batch, length, d * ATTN_W), BF16),
                   jax.ShapeDtypeStruct((batch, length, d * LANES), F32)),
        grid=(batch, d // nres, length // tq),
        in_specs=[cur, prev, cur, nxt, prev, cur, nxt],
        out_specs=(cur, pl.BlockSpec((None, tq, nres * LANES), lambda b, r, j: (b, j, r))),
        scratch_shapes=[pltpu.VMEM((tq + 2 * HALF, width), BF16),
                        pltpu.VMEM((tq + 2 * HALF, width), BF16)],
        compiler_params=_cparams("arbitrary", "arbitrary", "arbitrary"),
        name=f"attention_d{d}",
    )(view(q), view(k), view(k), view(k), view(v), view(v), view(v))
    rows = batch * length
    return o.reshape(rows, d * ATTN_W), lse.reshape(rows, d * LANES)


POST_TM = 256
CONV_HALO = 16
CONV_ROWS = 64
ROUTER_W = 128
ROW_WORDS = D_MODEL // 2


def _post_kernel(x_ref, mod_ref, o1_ref, o4_ref, o16_ref, l1_ref, l4_ref, l16_ref,
                 zp_ref, zc_ref, zn_ref, cw_ref, cb_ref, lng_ref, lnb_ref,
                 wo_ref, g2_ref, wr_ref, br_ref, cnt0_ref,
                 xn_ref, h2_ref, e_ref, r_ref, gt_ref, cnt_ref,
                 zw_ref, zs_ref, cat_ref, carry_ref, on_ref, ln_ref, *, tiles_per_row):
    tm = POST_TM
    i = pl.program_id(0)

    @pl.when(i == 0)
    def _():
        carry_ref[...] = cnt0_ref[...]

    pos = i % tiles_per_row
    zp = zp_ref[...].astype(F32)
    zn = zn_ref[...].astype(F32)
    zw_ref[0:CONV_HALO, :] = jnp.where(pos == 0, 0.0, zp)
    zw_ref[CONV_HALO:CONV_HALO + tm, :] = zc_ref[...].astype(F32)
    zw_ref[CONV_HALO + tm:, :] = jnp.where(pos == tiles_per_row - 1, 0.0, zn)
    cw = cw_ref[...]
    off = CONV_HALO - CONV_PAD
    span = tm + 2 * CONV_HALO - SUBLANES
    for s in range(SUBLANES):
        zs_ref[s, 0:span, :] = zw_ref[s:s + span, :]
    for rc in range(tm // CONV_ROWS):
        r0 = rc * CONV_ROWS
        acc = jnp.broadcast_to(cb_ref[...], (CONV_ROWS, CONV_W))
        for tap in range(CONV_K):
            q, s = divmod(off + tap, SUBLANES)
            lo = r0 + q * SUBLANES
            acc = acc + zs_ref[s, lo:lo + CONV_ROWS, :] * cw[tap:tap + 1, :]
        mu = jnp.mean(acc, axis=-1, keepdims=True)
        cen = acc - mu
        var = jnp.mean(cen * cen, axis=-1, keepdims=True)
        zf = cen * lax.rsqrt(var + EPS) * lng_ref[...] + lnb_ref[...]
        cat_ref[r0:r0 + CONV_ROWS, ATTN_W:] = (zf * jax.nn.sigmoid(zf)).astype(BF16)

    n_chunks = ATTN_W // LANES
    for pi, (d, o_ref, l_ref) in enumerate(((DILATIONS[1], o4_ref, l4_ref), (DILATIONS[2], o16_ref, l16_ref))):
        rows = tm // d
        for r in range(d):
            ln_ref[pi, pl.ds(r, rows, stride=d), :] = l_ref[:, r * LANES:(r + 1) * LANES]
            for c in range(n_chunks):
                lo = r * ATTN_W + c * LANES
                on_ref[pi, c, pl.ds(r, rows, stride=d), :] = o_ref[:, lo:lo + LANES].astype(F32)

    la, lb, lc = l1_ref[...], ln_ref[0], ln_ref[1]
    lmax = jnp.maximum(jnp.maximum(la, lb), lc)
    wa, wb, wc = jnp.exp(la - lmax), jnp.exp(lb - lmax), jnp.exp(lc - lmax)
    inv = 1.0 / (wa + wb + wc)
    wa, wb, wc = wa * inv, wb * inv, wc * inv
    first_head = lax.broadcasted_iota(jnp.int32, (tm, LANES), 1) < HEAD_DIM
    per_chunk = LANES // HEAD_DIM
    for c in range(n_chunks):
        h0 = c * per_chunk
        spread = lambda w: jnp.where(first_head, w[:, h0:h0 + 1], w[:, h0 + 1:h0 + 2])
        lo = c * LANES
        attn = (spread(wa) * o1_ref[:, lo:lo + LANES].astype(F32)
                + spread(wb) * on_ref[0, c] + spread(wc) * on_ref[1, c])
        cat_ref[:, lo:lo + LANES] = attn.astype(BF16)

    mix = jnp.dot(cat_ref[...], wo_ref[...], preferred_element_type=F32)
    gate1 = mod_ref[2:3, :]
    xn = x_ref[...] + gate1 * mix
    xn_ref[...] = xn
    h2 = _rmsnorm_rows(xn, g2_ref[...]) * (1.0 + mod_ref[4:5, :]) + mod_ref[3:4, :]

    hi = h2.astype(BF16)
    hi_f = hi.astype(F32)
    h2_ref[...] = _pack_halves(hi_f)
    lo = (h2 - hi_f).astype(BF16)
    wr = wr_ref[...]
    both = jnp.dot(hi, wr, preferred_element_type=F32) + jnp.dot(lo, wr, preferred_element_type=F32)
    bt = both.T
    logit = bt[0:N_EXPERTS, :] + bt[N_EXPERTS:2 * N_EXPERTS, :] + br_ref[...]

    row = lax.broadcasted_iota(jnp.int32, (N_EXPERTS, tm), 0).astype(F32)
    work = logit
    sel = jnp.zeros((N_EXPERTS, tm), F32)
    idxs, vals, hots = [], [], []
    for _ in range(TOP_K):
        m = jnp.max(work, axis=0, keepdims=True)
        idx = jnp.min(jnp.where(work == m, row, float(N_EXPERTS)), axis=0, keepdims=True)
        hot = row == idx
        work = jnp.where(hot, -jnp.inf, work)
        sel = jnp.where(hot, 1.0, sel)
        idxs.append(idx.astype(jnp.int32))
        vals.append(m)
        hots.append(hot)
    exps = [jnp.exp(v - vals[0]) for v in vals]
    esum = exps[0] + exps[1] + exps[2] + exps[3]
    gates = [e / esum for e in exps]

    tr = lax.broadcasted_iota(jnp.int32, (tm, tm), 0)
    tc = lax.broadcasted_iota(jnp.int32, (tm, tm), 1)
    before = jnp.where(tr < tc, 1.0, 0.0).astype(BF16)
    cnt = jnp.dot(sel.astype(BF16), before, preferred_element_type=F32) + carry_ref[:, 0:1]
    ranks = [jnp.sum(jnp.where(h, cnt, 0.0), axis=0, keepdims=True).astype(jnp.int32) for h in hots]
    carry_ref[...] = carry_ref[...] + jnp.sum(sel, axis=1, keepdims=True)
    cnt_ref[...] = carry_ref[...]

    row8 = lax.broadcasted_iota(jnp.int32, (8, tm), 0)
    e8 = jnp.zeros((8, tm), jnp.int32)
    r8 = jnp.zeros((8, tm), jnp.int32)
    for k in range(TOP_K):
        e8 = jnp.where(row8 == k, idxs[k], e8)
        r8 = jnp.where(row8 == k, ranks[k], r8)
    e_ref[...] = e8
    r_ref[...] = r8
    rowg = lax.broadcasted_iota(jnp.int32, (LANES, tm), 0)
    g_t = jnp.zeros((LANES, tm), F32)
    for k in range(TOP_K):
        g_t = jnp.where(rowg == k, gates[k], g_t)
    gt_ref[...] = g_t.T


def _post(x2, mod8, mod_off, attn_outs, z, conv_w, conv_b, ln_g, ln_b, w_out, g2, w_router2, b_router_t,
          cnt0, batch, seq):
    tm = POST_TM
    tokens = batch * seq
    tiles_per_row = seq // tm
    nh = tm // CONV_HALO
    last_h = tokens // CONV_HALO - 1
    row = lambda w: pl.BlockSpec((tm, w), lambda i: (i, 0))
    full = lambda a, b: pl.BlockSpec((a, b), lambda i: (0, 0))
    (o1, l1), (o4, l4), (o16, l16) = attn_outs
    d4, d16 = DILATIONS[1], DILATIONS[2]
    strided = lambda d, w: pl.BlockSpec((tm // d, d * w), lambda i: (i, 0))
    out_shapes = (jax.ShapeDtypeStruct((tokens, D_MODEL), F32),
                  jax.ShapeDtypeStruct((tokens, ROW_WORDS), jnp.uint32),
                  jax.ShapeDtypeStruct((8, tokens), jnp.int32),
                  jax.ShapeDtypeStruct((8, tokens), jnp.int32),
                  jax.ShapeDtypeStruct((tokens, LANES), F32),
                  jax.ShapeDtypeStruct((N_EXPERTS, LANES), F32))
    return pl.pallas_call(
        functools.partial(_post_kernel, tiles_per_row=tiles_per_row),
        out_shape=out_shapes,
        grid=(tokens // tm,),
        in_specs=[row(D_MODEL),
                  pl.BlockSpec((None, 8, D_MODEL), lambda i: (mod_off + i // tiles_per_row, 0, 0)),
                  row(ATTN_W), strided(d4, ATTN_W), strided(d16, ATTN_W),
                  row(LANES), strided(d4, LANES), strided(d16, LANES),
                  pl.BlockSpec((CONV_HALO, CONV_W), lambda i: (jnp.maximum(i * nh - 1, 0), 0)),
                  row(CONV_W),
                  pl.BlockSpec((CONV_HALO, CONV_W), lambda i: (jnp.minimum((i + 1) * nh, last_h), 0)),
                  full(32, CONV_W), full(1, CONV_W), full(1, CONV_W), full(1, CONV_W),
                  full(D_MODEL, D_MODEL), full(1, D_MODEL), full(D_MODEL, ROUTER_W),
                  full(N_EXPERTS, tm), full(N_EXPERTS, LANES)],
        out_specs=(row(D_MODEL), row(ROW_WORDS),
                   pl.BlockSpec((8, tm), lambda i: (0, i)), pl.BlockSpec((8, tm), lambda i: (0, i)),
                   row(LANES), full(N_EXPERTS, LANES)),
        scratch_shapes=[pltpu.VMEM((tm + 2 * CONV_HALO, CONV_W), F32),
                        pltpu.VMEM((SUBLANES, tm + 2 * CONV_HALO, CONV_W), F32),
                        pltpu.VMEM((tm, ATTN_W + CONV_W), BF16),
                        pltpu.VMEM((N_EXPERTS, LANES), F32),
                        pltpu.VMEM((2, ATTN_W // LANES, tm, LANES), F32),
                        pltpu.VMEM((2, tm, LANES), F32)],
        compiler_params=_cparams("arbitrary"),
        name="post",
    )(x2, mod8, o1, o4, o16, l1, l4, l16, z, z, z, conv_w, conv_b, ln_g, ln_b, w_out, g2,
      w_router2, b_router_t, cnt0)


MOE_BM = 256

SLOT_TL = 2048
SC_CORES = 2
SC_SUBCORES = 16
SC_WORKERS = SC_CORES * SC_SUBCORES
SC_CHUNK = 64


def _slot_kernel(plan_ref, e_ref, r_ref, s_ref):
    e = e_ref[...]
    start = jnp.zeros(e.shape, jnp.int32)
    for x in range(N_EXPERTS):
        start = jnp.where(e == x, plan_ref[x], start)
    s_ref[...] = start + r_ref[...]


def _slots(plan, e8, r8):
    tokens = e8.shape[1]
    spec = pl.BlockSpec((8, SLOT_TL), lambda i, ps: (0, i))
    s8 = pl.pallas_call(
        _slot_kernel,
        out_shape=jax.ShapeDtypeStruct((8, tokens), jnp.int32),
        grid_spec=pltpu.PrefetchScalarGridSpec(
            num_scalar_prefetch=1, grid=(tokens // SLOT_TL,), in_specs=[spec, spec], out_specs=spec),
        compiler_params=_cparams("arbitrary"),
        name="slots",
    )(plan, e8, r8)
    return s8[:TOP_K].reshape(TOP_K, tokens // SC_CHUNK, SC_CHUNK).transpose(1, 0, 2)


def _sc_worker_chunks(tokens):
    return tokens // (SC_WORKERS * SC_CHUNK)


def _sc_mesh():
    return plsc.VectorSubcoreMesh(core_axis_name="c", subcore_axis_name="s")


def _dispatch(slots, h2, n_pad):
    tokens = h2.shape[0]
    n_chunks = _sc_worker_chunks(tokens)

    @functools.partial(
        pl.kernel, mesh=_sc_mesh(),
        out_type=jax.ShapeDtypeStruct((n_pad, ROW_WORDS), jnp.uint32),
        scratch_types=[pltpu.VMEM((TOP_K, SC_CHUNK), jnp.int32),
                       pltpu.VMEM((SC_CHUNK, ROW_WORDS), jnp.uint32),
                       pltpu.SemaphoreType.DMA],
        name="sc_dispatch")
    def scatter(h_hbm, slot_hbm, rows_hbm, idx_v, rows_v, sem):
        worker = lax.axis_index("s") * SC_CORES + lax.axis_index("c")

        @pl.loop(0, n_chunks)
        def _(ci):
            chunk = worker * n_chunks + ci
            pltpu.sync_copy(slot_hbm.at[chunk], idx_v)
            pltpu.sync_copy(h_hbm.at[pl.ds(chunk * SC_CHUNK, SC_CHUNK)], rows_v)
            copies = [pltpu.async_copy(rows_v, rows_hbm.at[idx_v.at[k]], sem) for k in range(TOP_K)]
            for cp in copies:
                cp.wait()

    return scatter(h2, slots)


FF_CHUNK = 1024


CAST_ROWS = 64


def _experts_kernel(nblk_ref, bstart_ref, x_hbm, wu_ref, bu_ref, wd_ref, bd_ref, y_hbm,
                    wub_ref, wdb_ref, xbuf_ref, ybuf_ref, sem_in, sem_out):
    e = pl.program_id(0)
    n = nblk_ref[e]
    b0 = bstart_ref[e]
    n_full = n // 2
    odd = n % 2
    big = 2 * MOE_BM

    def rows_at(ref, u, rows):
        return ref.at[pl.ds(pl.multiple_of((b0 + 2 * u) * MOE_BM, MOE_BM), rows)]

    def x_copy(u, rows):
        slot = u % 2
        return pltpu.make_async_copy(rows_at(x_hbm, u, rows), xbuf_ref.at[slot, pl.ds(0, rows)],
                                     sem_in.at[slot])

    def y_copy(u, rows):
        slot = u % 2
        return pltpu.make_async_copy(ybuf_ref.at[slot, pl.ds(0, rows)], rows_at(y_hbm, u, rows),
                                     sem_out.at[slot])

    def start_load(u):
        @pl.when(u < n_full)
        def _():
            x_copy(u, big).start()

        @pl.when((u == n_full) & (odd == 1))
        def _():
            x_copy(u, MOE_BM).start()

    def compute(u, rows):
        slot = u % 2
        x_lo, x_hi = _unpack_halves(xbuf_ref[slot, pl.ds(0, rows), :])
        xb = jnp.concatenate([x_lo.astype(BF16), x_hi.astype(BF16)], axis=1)
        acc = jnp.broadcast_to(bd_ref[...], (rows, D_MODEL))
        for c in range(D_FF // FF_CHUNK):
            lo = c * FF_CHUNK
            glu = (jnp.dot(xb, wub_ref[:, lo:lo + FF_CHUNK], preferred_element_type=F32)
                   + bu_ref[:, lo:lo + FF_CHUNK])
            lin = (jnp.dot(xb, wub_ref[:, D_FF + lo:D_FF + lo + FF_CHUNK], preferred_element_type=F32)
                   + bu_ref[:, D_FF + lo:D_FF + lo + FF_CHUNK])
            glu = jnp.minimum(glu, SWIGLU_LIMIT)
            lin = jnp.clip(lin, -SWIGLU_LIMIT, SWIGLU_LIMIT)
            act = glu * jax.nn.sigmoid(SWIGLU_ALPHA * glu) * (lin + 1.0)
            acc = acc + jnp.dot(act.astype(BF16), wdb_ref[lo:lo + FF_CHUNK, :], preferred_element_type=F32)
        ybuf_ref[slot, pl.ds(0, rows), :] = _pack_halves(acc.astype(BF16).astype(F32))

    @pl.when(n > 0)
    def _():
        start_load(0)

        def cast_rows(i, carry):
            r = pl.multiple_of(i * CAST_ROWS, CAST_ROWS)
            wub_ref[pl.ds(r, CAST_ROWS), :] = wu_ref[pl.ds(r, CAST_ROWS), :].astype(BF16)
            wdb_ref[pl.ds(r, CAST_ROWS), :] = wd_ref[pl.ds(r, CAST_ROWS), :].astype(BF16)
            return carry

        lax.fori_loop(0, D_MODEL // CAST_ROWS, cast_rows, 0)

        def full_unit(u, carry):
            x_copy(u, big).wait()
            start_load(u + 1)

            @pl.when(u >= 2)
            def _():
                y_copy(u - 2, big).wait()

            compute(u, big)
            y_copy(u, big).start()
            return carry

        lax.fori_loop(0, n_full, full_unit, 0)

        @pl.when(n_full >= 2)
        def _():
            y_copy(n_full - 2, big).wait()

        @pl.when(odd == 1)
        def _():
            x_copy(n_full, MOE_BM).wait()
            compute(n_full, MOE_BM)
            y_copy(n_full, MOE_BM).start()

        @pl.when(n_full >= 1)
        def _():
            y_copy(n_full - 1, big).wait()

        @pl.when(odd == 1)
        def _():
            y_copy(n_full, MOE_BM).wait()


def _experts(nblk, bstart, rows, w_up, b_up, w_down, b_down):
    n_pad = rows.shape[0]
    per_expert = lambda *tail: pl.BlockSpec((None, *tail), lambda e, nb, bs: (e, 0, 0))
    return pl.pallas_call(
        _experts_kernel,
        out_shape=jax.ShapeDtypeStruct((n_pad, ROW_WORDS), jnp.uint32),
        grid_spec=pltpu.PrefetchScalarGridSpec(
            num_scalar_prefetch=2, grid=(N_EXPERTS,),
            in_specs=[pl.BlockSpec(memory_space=pl.ANY),
                      per_expert(D_MODEL, 2 * D_FF), per_expert(1, 2 * D_FF),
                      per_expert(D_FF, D_MODEL), per_expert(1, D_MODEL)],
            out_specs=pl.BlockSpec(memory_space=pl.ANY),
            scratch_shapes=[pltpu.VMEM((D_MODEL, 2 * D_FF), BF16), pltpu.VMEM((D_FF, D_MODEL), BF16),
                            pltpu.VMEM((2, 2 * MOE_BM, ROW_WORDS), jnp.uint32),
                            pltpu.VMEM((2, 2 * MOE_BM, ROW_WORDS), jnp.uint32),
                            pltpu.SemaphoreType.DMA((2,)), pltpu.SemaphoreType.DMA((2,))]),
        compiler_params=_cparams("arbitrary"),
        name="experts",
    )(nblk, bstart, rows, w_up, b_up, w_down, b_down)


COMBINE_TM = 256


def _gather_rows(slots, y_rows, tokens):
    n_chunks = _sc_worker_chunks(tokens)

    @functools.partial(
        pl.kernel, mesh=_sc_mesh(),
        out_type=jax.ShapeDtypeStruct((TOP_K, tokens, ROW_WORDS), jnp.uint32),
        scratch_types=[pltpu.VMEM((TOP_K, SC_CHUNK), jnp.int32),
                       pltpu.VMEM((SC_CHUNK, ROW_WORDS), jnp.uint32),
                       pltpu.SemaphoreType.DMA],
        name="sc_gather")
    def gather(y_hbm, slot_hbm, out_hbm, idx_v, rows_v, sem):
        worker = lax.axis_index("s") * SC_CORES + lax.axis_index("c")

        @pl.loop(0, n_chunks)
        def _(ci):
            chunk = worker * n_chunks + ci
            pltpu.sync_copy(slot_hbm.at[chunk], idx_v)
            for k in range(TOP_K):
                pltpu.async_copy(y_hbm.at[idx_v.at[k]], rows_v, sem).wait()
                pltpu.sync_copy(rows_v, out_hbm.at[k, pl.ds(chunk * SC_CHUNK, SC_CHUNK)])

    return gather(y_rows, slots)


def _combine_kernel(xn_ref, gt_ref, mod_ref, fg_ref, rows_ref, out_ref):
    gt = gt_ref[...]
    y_lo = y_hi = None
    for k in range(TOP_K):
        lo, hi = _unpack_halves(rows_ref[k])
        g = gt[:, k:k + 1]
        y_lo = g * lo if y_lo is None else y_lo + g * lo
        y_hi = g * hi if y_hi is None else y_hi + g * hi
    y = jnp.concatenate([y_lo, y_hi], axis=1)
    xo = xn_ref[...] + mod_ref[5:6, :] * y
    out_ref[...] = _rmsnorm_rows(xo, fg_ref[...])


def _combine(xn, gt, mod8, mod_off, final_g, gathered, batch, seq):
    tm = COMBINE_TM
    tokens = batch * seq
    tiles_per_row = seq // tm
    return pl.pallas_call(
        _combine_kernel,
        out_shape=jax.ShapeDtypeStruct((tokens, D_MODEL), F32),
        grid=(tokens // tm,),
        in_specs=[pl.BlockSpec((tm, D_MODEL), lambda i: (i, 0)),
                  pl.BlockSpec((tm, LANES), lambda i: (i, 0)),
                  pl.BlockSpec((None, 8, D_MODEL), lambda i: (mod_off + i // tiles_per_row, 0, 0)),
                  pl.BlockSpec((1, D_MODEL), lambda i: (0, 0)),
                  pl.BlockSpec((TOP_K, tm, ROW_WORDS), lambda i: (0, i, 0))],
        out_specs=pl.BlockSpec((tm, D_MODEL), lambda i: (i, 0)),
        compiler_params=_cparams("arbitrary"),
        name="combine",
    )(xn, gt, mod8, final_g, gathered)


def _block_plan(counts):
    nblk = (counts + MOE_BM - 1) // MOE_BM
    bstart = jnp.cumsum(nblk) - nblk
    return (bstart * MOE_BM).astype(jnp.int32), nblk.astype(jnp.int32), bstart.astype(jnp.int32)


def _run_group(x, mod8, mod_off, p):
    batch, seq, _ = x.shape
    tokens = batch * seq
    x2 = x.reshape(tokens, D_MODEL)
    qkv, z = _in_proj(x2, mod8, mod_off, p["g1"], p["w_in"], _rope_tables(seq), batch, seq, tm=512)
    attn_outs = [_attention_pattern(*qkv_d, batch, seq, d) for qkv_d, d in zip(qkv, DILATIONS)]
    cnt0 = jnp.zeros((N_EXPERTS, LANES), F32)
    xn, h2, e8, r8, gt, cnt = _post(x2, mod8, mod_off, attn_outs, z, p["conv_w"], p["conv_b"], p["ln_g"],
                                    p["ln_b"], p["w_out"], p["g2"], p["w_router2"], p["b_router_t"],
                                    cnt0, batch, seq)
    n_pad = tokens * TOP_K + N_EXPERTS * MOE_BM
    pstart, nblk, bstart = _block_plan(cnt[:, 0].astype(jnp.int32))
    slots = _slots(pstart, e8, r8)
    rows = _dispatch(slots, h2, n_pad)
    y_rows = _experts(nblk, bstart, rows, p["w_up"], p["b_up"], p["w_down"], p["b_down"])
    gathered = _gather_rows(slots, y_rows, tokens)
    out = _combine(xn, gt, mod8, mod_off, p["final_g"], gathered, batch, seq)
    return out.reshape(batch, seq, D_MODEL)


def _prepare(w_ada, b_ada, norm_mix_g, w_in, conv_w, conv_b, conv_ln_g, conv_ln_b, w_out, norm_ffn_g,
             w_router, b_router, w_up, b_up, w_down, b_down, final_g):
    wr = w_router[0]
    wr_hi = wr.astype(BF16)
    wr_lo = (wr - wr_hi.astype(F32)).astype(BF16)
    w_router2 = jnp.concatenate(
        [wr_hi, wr_lo, jnp.zeros((D_MODEL, ROUTER_W - 2 * N_EXPERTS), BF16)], axis=1)
    return dict(
        g1=norm_mix_g[0][None, :], w_in=w_in[0].astype(BF16),
        conv_w=jnp.concatenate([conv_w[0], jnp.zeros((32 - CONV_K, CONV_W), F32)], axis=0),
        conv_b=conv_b[0][None, :], ln_g=conv_ln_g[0][None, :], ln_b=conv_ln_b[0][None, :],
        w_out=w_out[0].astype(BF16), g2=norm_ffn_g[0][None, :],
        w_router2=w_router2,
        b_router_t=jnp.broadcast_to(b_router[0][:, None], (N_EXPERTS, POST_TM)),
        w_up=w_up[0], b_up=b_up[0][:, None, :],
        w_down=w_down[0], b_down=b_down[0][:, None, :],
        final_g=final_g[None, :])


def kernel(x_prompt, x_sample, c_prompt, c_sample, w_ada, b_ada, norm_mix_g, w_in, conv_w, conv_b,
           conv_ln_g, conv_ln_b, w_out, norm_ffn_g, w_router, b_router, w_up, b_up, w_down, b_down,
           final_g):
    p = _prepare(w_ada, b_ada, norm_mix_g, w_in, conv_w, conv_b, conv_ln_g, conv_ln_b, w_out,
                 norm_ffn_g, w_router, b_router, w_up, b_up, w_down, b_down, final_g)
    nb_p, nb_s = c_prompt.shape[0], c_sample.shape[0]
    c_rows = 16
    c_all = jnp.concatenate([c_prompt, c_sample, jnp.zeros((c_rows - nb_p - nb_s, D_MODEL), F32)], axis=0)
    mod = _modulation(c_all, w_ada[0].astype(BF16), b_ada[0][None, :])
    mod8 = jnp.concatenate([mod.reshape(c_rows, 6, D_MODEL), jnp.zeros((c_rows, 2, D_MODEL), F32)], axis=1)
    y_prompt = _run_group(x_prompt, mod8, 0, p)
    y_sample = _run_group(x_sample, mod8, nb_p, p)
    return (y_prompt, y_sample)
```

```python
import functools

import jax
import jax.numpy as jnp
from jax import lax
from jax.experimental import pallas as pl
from jax.experimental.pallas import tpu as pltpu
from jax.experimental.pallas import tpu_sc as plsc

D_MODEL = 1024
N_HEADS = 8
HEAD_DIM = 64
ATTN_W = N_HEADS * HEAD_DIM
CONV_W = D_MODEL // 2
IN_PROJ_W = 3 * ATTN_W + 2 * CONV_W
WINDOWS = (128, 512, 2048)
DILATIONS = (1, 4, 16)
ATTN_BLOCK = 128
HALF = 64
ROPE_DIM = HEAD_DIM // 4
ROPE_THETA = 500000.0
NEG_INF = -1e30
CONV_K = 31
CONV_PAD = CONV_K // 2
N_EXPERTS = 32
TOP_K = 4
D_FF = D_MODEL
SWIGLU_ALPHA = 1.702
SWIGLU_LIMIT = 7.0
EPS = 1e-6

LANES = 128
SUBLANES = 8
VMEM_LIMIT = 56 * 1024 * 1024

F32 = jnp.float32
BF16 = jnp.bfloat16


def _cparams(*sem):
    return pltpu.CompilerParams(dimension_semantics=sem, vmem_limit_bytes=VMEM_LIMIT)


def _mod_kernel(c_ref, w_ref, b_ref, o_ref):
    c = c_ref[...]
    a = (c * jax.nn.sigmoid(c)).astype(BF16)
    o_ref[...] = jnp.dot(a, w_ref[...], preferred_element_type=F32) + b_ref[...]


def _modulation(c_all, w_ada, b_ada):
    rows = c_all.shape[0]
    n = w_ada.shape[1]
    tn = 1536
    return pl.pallas_call(
        _mod_kernel,
        out_shape=jax.ShapeDtypeStruct((rows, n), F32),
        grid=(n // tn,),
        in_specs=[pl.BlockSpec((rows, D_MODEL), lambda j: (0, 0)),
                  pl.BlockSpec((D_MODEL, tn), lambda j: (0, j)),
                  pl.BlockSpec((1, tn), lambda j: (0, j))],
        out_specs=pl.BlockSpec((rows, tn), lambda j: (0, j)),
        compiler_params=_cparams("arbitrary"),
        name="modulation",
    )(c_all, w_ada, b_ada)


def _rmsnorm_rows(x, g):
    return x * lax.rsqrt(jnp.mean(x * x, axis=-1, keepdims=True) + EPS) * g


def _pack_halves(x):
    n = x.shape[1] // 2
    lo = lax.bitcast_convert_type(x[:, :n], jnp.uint32)
    hi = lax.bitcast_convert_type(x[:, n:], jnp.uint32)
    return (lo >> 16) | (hi & jnp.uint32(0xFFFF0000))


def _unpack_halves(p):
    lo = lax.bitcast_convert_type(p << 16, F32)
    hi = lax.bitcast_convert_type(p & jnp.uint32(0xFFFF0000), F32)
    return lo, hi


def _inproj_kernel(x_ref, mod_ref, g_ref, w_ref, cos_ref, s1_ref, s2_ref,
                   q1_ref, k1_ref, v1_ref, q4_ref, k4_ref, v4_ref, q16_ref, k16_ref, v16_ref,
                   z_ref, stage_ref):
    tm = x_ref.shape[0]
    x = x_ref[...]
    shift = mod_ref[0:1, :]
    scale = mod_ref[1:2, :]
    h = _rmsnorm_rows(x, g_ref[...]) * (1.0 + scale) + shift
    hb = h.astype(BF16)
    cos = cos_ref[...]
    s1 = s1_ref[...]
    s2 = s2_ref[...]

    def rope(p):
        return p * cos + pltpu.roll(p, LANES - ROPE_DIM // 2, 1) * s1 + pltpu.roll(p, ROPE_DIM // 2, 1) * s2

    n_chunks = ATTN_W // LANES
    plans = ((0, lambda p: rope(p) * (HEAD_DIM ** -0.5), (q1_ref, q4_ref, q16_ref)),
             (ATTN_W, rope, (k1_ref, k4_ref, k16_ref)),
             (2 * ATTN_W, lambda p: p, (v1_ref, v4_ref, v16_ref)))
    for col0, finish, (nat_ref, *strided_refs) in plans:
        proj = jnp.dot(hb, w_ref[:, col0:col0 + ATTN_W], preferred_element_type=F32)
        for c in range(n_chunks):
            lo = c * LANES
            val = finish(proj[:, lo:lo + LANES])
            nat_ref[:, lo:lo + LANES] = val.astype(BF16)
            stage_ref[c] = val
        for d, out_ref in zip(DILATIONS[1:], strided_refs):
            for r in range(d):
                for c in range(n_chunks):
                    lo = r * ATTN_W + c * LANES
                    out_ref[:, lo:lo + LANES] = stage_ref[c, pl.ds(r, tm // d, stride=d), :].astype(BF16)
    a = jnp.dot(hb, w_ref[:, 3 * ATTN_W:3 * ATTN_W + CONV_W], preferred_element_type=F32)
    g = jnp.dot(hb, w_ref[:, 3 * ATTN_W + CONV_W:], preferred_element_type=F32)
    z_ref[...] = (a * jax.nn.sigmoid(g)).astype(BF16)


def _rope_tables(seq):
    half = ROPE_DIM // 2
    inv_freq = ROPE_THETA ** (-jnp.arange(half, dtype=F32) * 2.0 / ROPE_DIM)
    ang = jnp.arange(seq, dtype=F32)[:, None] * inv_freq[None, :]
    cos = jnp.cos(ang)
    sin = jnp.sin(ang)
    ones = jnp.ones((seq, HEAD_DIM - ROPE_DIM), F32)
    zeros = jnp.zeros((seq, HEAD_DIM - ROPE_DIM), F32)
    zh = jnp.zeros((seq, half), F32)
    cos_h = jnp.concatenate([cos, cos, ones], axis=1)
    s1_h = jnp.concatenate([-sin, zh, zeros], axis=1)
    s2_h = jnp.concatenate([zh, sin, zeros], axis=1)
    rep = LANES // HEAD_DIM
    return tuple(jnp.tile(t, (1, rep)) for t in (cos_h, s1_h, s2_h))


def _in_proj(x2, mod8, mod_off, g1, w_in, tables, batch, seq, tm):
    tokens = batch * seq
    tiles_per_row = seq // tm
    row_spec = lambda w: pl.BlockSpec((tm, w), lambda i: (i, 0))
    tab_spec = pl.BlockSpec((tm, LANES), lambda i: (i % tiles_per_row, 0))
    qkv_shapes, qkv_specs = [], []
    for d in DILATIONS:
        qkv_shapes += [jax.ShapeDtypeStruct((tokens // d, d * ATTN_W), BF16)] * 3
        qkv_specs += [pl.BlockSpec((tm // d, d * ATTN_W), lambda i: (i, 0))] * 3
    outs = pl.pallas_call(
        _inproj_kernel,
        out_shape=(*qkv_shapes, jax.ShapeDtypeStruct((tokens, CONV_W), BF16)),
        grid=(tokens // tm,),
        in_specs=[row_spec(D_MODEL),
                  pl.BlockSpec((None, 8, D_MODEL), lambda i: (mod_off + i // tiles_per_row, 0, 0)),
                  pl.BlockSpec((1, D_MODEL), lambda i: (0, 0)),
                  pl.BlockSpec((D_MODEL, IN_PROJ_W), lambda i: (0, 0)),
                  tab_spec, tab_spec, tab_spec],
        out_specs=(*qkv_specs, row_spec(CONV_W)),
        scratch_shapes=[pltpu.VMEM((ATTN_W // LANES, tm, LANES), F32)],
        compiler_params=_cparams("arbitrary"),
        name="in_proj",
    )(x2, mod8, g1, w_in, *tables)
    qkv = [outs[3 * n:3 * n + 3] for n in range(len(DILATIONS))]
    return qkv, outs[-1]


def _attn_kernel(q_ref, kp_ref, kc_ref, kn_ref, vp_ref, vc_ref, vn_ref, o_ref, lse_ref,
                 kw_ref, vw_ref, *, tq, length, nres):
    j = pl.program_id(2)
    kw_ref[0:HALF, :] = kp_ref[...]
    kw_ref[HALF:HALF + tq, :] = kc_ref[...]
    kw_ref[HALF + tq:, :] = kn_ref[...]
    vw_ref[0:HALF, :] = vp_ref[...]
    vw_ref[HALF:HALF + tq, :] = vc_ref[...]
    vw_ref[HALF + tq:, :] = vn_ref[...]

    nk = 2 * ATTN_BLOCK
    t_io = lax.broadcasted_iota(jnp.int32, (ATTN_BLOCK, nk), 0)
    u_io = lax.broadcasted_iota(jnp.int32, (ATTN_BLOCK, nk), 1)
    lane = lax.broadcasted_iota(jnp.int32, (ATTN_BLOCK, LANES), 1)
    first_head = lane < HEAD_DIM
    lane1 = lax.broadcasted_iota(jnp.int32, (1, LANES), 1)
    head_keep = [jnp.where(lane1 < HEAD_DIM, 1.0, 0.0).astype(BF16),
                 jnp.where(lane1 < HEAD_DIM, 0.0, 1.0).astype(BF16)]

    for blk in range(tq // ATTN_BLOCK):
        r0 = blk * ATTN_BLOCK
        base = j * tq + r0
        u_min = jnp.maximum(t_io, HALF - base)
        u_max = jnp.minimum(t_io + 2 * HALF, length + HALF - 1 - base)
        mask = (u_io >= u_min) & (u_io <= u_max)
        for res in range(nres):
            lse_tile = jnp.zeros((ATTN_BLOCK, LANES), F32)
            for c in range(ATTN_W // LANES):
                lo = res * ATTN_W + c * LANES
                qc = q_ref[r0:r0 + ATTN_BLOCK, lo:lo + LANES]
                kc = kw_ref[r0:r0 + nk, lo:lo + LANES]
                vc = vw_ref[r0:r0 + nk, lo:lo + LANES]
                outs = []
                for hh in range(LANES // HEAD_DIM):
                    qm = qc * head_keep[hh]
                    s = lax.dot_general(qm, kc, (((1,), (1,)), ((), ())), preferred_element_type=F32)
                    s = jnp.where(mask, s, NEG_INF)
                    m = jnp.max(s, axis=-1, keepdims=True)
                    p = jnp.exp(s - m)
                    den = jnp.sum(p, axis=-1, keepdims=True)
                    pv = jnp.dot(p.astype(BF16), vc, preferred_element_type=F32)
                    outs.append(pv / den)
                    lse_tile = jnp.where(lane == c * (LANES // HEAD_DIM) + hh, m + jnp.log(den), lse_tile)
                o_ref[r0:r0 + ATTN_BLOCK, lo:lo + LANES] = jnp.where(first_head, outs[0], outs[1]).astype(BF16)
            lse_ref[r0:r0 + ATTN_BLOCK, res * LANES:(res + 1) * LANES] = lse_tile


ATTN_QUERIES = 512


def _attention_pattern(q, k, v, batch, seq, d):
    length = seq // d
    tq = min(length, ATTN_QUERIES)
    nres = min(d, ATTN_QUERIES // tq)
    view = lambda t: t.reshape(batch, length, d * ATTN_W)
    nh = tq // HALF
    last_h = length // HALF - 1
    width = nres * ATTN_W
    cur = pl.BlockSpec((None, tq, width), lambda b, r, j: (b, j, r))
    prev = pl.BlockSpec((None, HALF, width), lambda b, r, j: (b, jnp.maximum(j * nh - 1, 0), r))
    nxt = pl.BlockSpec((None, HALF, width), lambda b, r, j: (b, jnp.minimum((j + 1) * nh, last_h), r))
    o, lse = pl.pallas_call(
        functools.partial(_attn_kernel, tq=tq, length=length, nres=nres),
        out_shape=(jax.ShapeDtypeStruct((batch, length, d * ATTN_W), BF16),
                   jax.ShapeDtypeStruct((batch, length, d * LANES), F32)),
        grid=(batch, d // nres, length // tq),
        in_specs=[cur, prev, cur, nxt, prev, cur, nxt],
        out_specs=(cur, pl.BlockSpec((None, tq, nres * LANES), lambda b, r, j: (b, j, r))),
        scratch_shapes=[pltpu.VMEM((tq + 2 * HALF, width), BF16),
                        pltpu.VMEM((tq + 2 * HALF, width), BF16)],
        compiler_params=_cparams("arbitrary", "arbitrary", "arbitrary"),
        name=f"attention_d{d}",
    )(view(q), view(k), view(k), view(k), view(v), view(v), view(v))
    rows = batch * length
    return o.reshape(rows, d * ATTN_W), lse.reshape(rows, d * LANES)


POST_TM = 256
CONV_HALO = 16
CONV_ROWS = 64
ROUTER_W = 128
ROW_WORDS = D_MODEL // 2


def _post_kernel(x_ref, mod_ref, o1_ref, o4_ref, o16_ref, l1_ref, l4_ref, l16_ref,
                 zp_ref, zc_ref, zn_ref, cw_ref, cb_ref, lng_ref, lnb_ref,
                 wo_ref, g2_ref, wr_ref, br_ref, cnt0_ref,
                 xn_ref, h2_ref, e_ref, r_ref, gt_ref, cnt_ref,
                 zw_ref, zs_ref, cat_ref, carry_ref, on_ref, ln_ref, *, tiles_per_row):
    tm = POST_TM
    i = pl.program_id(0)

    @pl.when(i == 0)
    def _():
        carry_ref[...] = cnt0_ref[...]

    pos = i % tiles_per_row
    zp = zp_ref[...].astype(F32)
    zn = zn_ref[...].astype(F32)
    zw_ref[0:CONV_HALO, :] = jnp.where(pos == 0, 0.0, zp)
    zw_ref[CONV_HALO:CONV_HALO + tm, :] = zc_ref[...].astype(F32)
    zw_ref[CONV_HALO + tm:, :] = jnp.where(pos == tiles_per_row - 1, 0.0, zn)
    cw = cw_ref[...]
    off = CONV_HALO - CONV_PAD
    span = tm + 2 * CONV_HALO - SUBLANES
    for s in range(SUBLANES):
        zs_ref[s, 0:span, :] = zw_ref[s:s + span, :]
    for rc in range(tm // CONV_ROWS):
        r0 = rc * CONV_ROWS
        acc = jnp.broadcast_to(cb_ref[...], (CONV_ROWS, CONV_W))
        for tap in range(CONV_K):
            q, s = divmod(off + tap, SUBLANES)
            lo = r0 + q * SUBLANES
            acc = acc + zs_ref[s, lo:lo + CONV_ROWS, :] * cw[tap:tap + 1, :]
        mu = jnp.mean(acc, axis=-1, keepdims=True)
        cen = acc - mu
        var = jnp.mean(cen * cen, axis=-1, keepdims=True)
        zf = cen * lax.rsqrt(var + EPS) * lng_ref[...] + lnb_ref[...]
        cat_ref[r0:r0 + CONV_ROWS, ATTN_W:] = (zf * jax.nn.sigmoid(zf)).astype(BF16)

    n_chunks = ATTN_W // LANES
    for pi, (d, o_ref, l_ref) in enumerate(((DILATIONS[1], o4_ref, l4_ref), (DILATIONS[2], o16_ref, l16_ref))):
        rows = tm // d
        for r in range(d):
            ln_ref[pi, pl.ds(r, rows, stride=d), :] = l_ref[:, r * LANES:(r + 1) * LANES]
            for c in range(n_chunks):
                lo = r * ATTN_W + c * LANES
                on_ref[pi, c, pl.ds(r, rows, stride=d), :] = o_ref[:, lo:lo + LANES].astype(F32)

    la, lb, lc = l1_ref[...], ln_ref[0], ln_ref[1]
    lmax = jnp.maximum(jnp.maximum(la, lb), lc)
    wa, wb, wc = jnp.exp(la - lmax), jnp.exp(lb - lmax), jnp.exp(lc - lmax)
    inv = 1.0 / (wa + wb + wc)
    wa, wb, wc = wa * inv, wb * inv, wc * inv
    first_head = lax.broadcasted_iota(jnp.int32, (tm, LANES), 1) < HEAD_DIM
    per_chunk = LANES // HEAD_DIM
    for c in range(n_chunks):
        h0 = c * per_chunk
        spread = lambda w: jnp.where(first_head, w[:, h0:h0 + 1], w[:, h0 + 1:h0 + 2])
        lo = c * LANES
        attn = (spread(wa) * o1_ref[:, lo:lo + LANES].astype(F32)
                + spread(wb) * on_ref[0, c] + spread(wc) * on_ref[1, c])
        cat_ref[:, lo:lo + LANES] = attn.astype(BF16)

    mix = jnp.dot(cat_ref[...], wo_ref[...], preferred_element_type=F32)
    gate1 = mod_ref[2:3, :]
    xn = x_ref[...] + gate1 * mix
    xn_ref[...] = xn
    h2 = _rmsnorm_rows(xn, g2_ref[...]) * (1.0 + mod_ref[4:5, :]) + mod_ref[3:4, :]

    hi = h2.astype(BF16)
    hi_f = hi.astype(F32)
    h2_ref[...] = _pack_halves(hi_f)
    lo = (h2 - hi_f).astype(BF16)
    wr = wr_ref[...]
    both = jnp.dot(hi, wr, preferred_element_type=F32) + jnp.dot(lo, wr, preferred_element_type=F32)
    bt = both.T
    logit = bt[0:N_EXPERTS, :] + bt[N_EXPERTS:2 * N_EXPERTS, :] + br_ref[...]

    row = lax.broadcasted_iota(jnp.int32, (N_EXPERTS, tm), 0).astype(F32)
    work = logit
    sel = jnp.zeros((N_EXPERTS, tm), F32)
    idxs, vals, hots = [], [], []
    for _ in range(TOP_K):
        m = jnp.max(work, axis=0, keepdims=True)
        idx = jnp.min(jnp.where(work == m, row, float(N_EXPERTS)), axis=0, keepdims=True)
        hot = row == idx
        work = jnp.where(hot, -jnp.inf, work)
        sel = jnp.where(hot, 1.0, sel)
        idxs.append(idx.astype(jnp.int32))
        vals.append(m)
        hots.append(hot)
    exps = [jnp.exp(v - vals[0]) for v in vals]
    esum = exps[0] + exps[1] + exps[2] + exps[3]
    gates = [e / esum for e in exps]

    tr = lax.broadcasted_iota(jnp.int32, (tm, tm), 0)
    tc = lax.broadcasted_iota(jnp.int32, (tm, tm), 1)
    before = jnp.where(tr < tc, 1.0, 0.0).astype(BF16)
    cnt = jnp.dot(sel.astype(BF16), before, preferred_element_type=F32) + carry_ref[:, 0:1]
    ranks = [jnp.sum(jnp.where(h, cnt, 0.0), axis=0, keepdims=True).astype(jnp.int32) for h in hots]
    carry_ref[...] = carry_ref[...] + jnp.sum(sel, axis=1, keepdims=True)
    cnt_ref[...] = carry_ref[...]

    row8 = lax.broadcasted_iota(jnp.int32, (8, tm), 0)
    e8 = jnp.zeros((8, tm), jnp.int32)
    r8 = jnp.zeros((8, tm), jnp.int32)
    for k in range(TOP_K):
        e8 = jnp.where(row8 == k, idxs[k], e8)
        r8 = jnp.where(row8 == k, ranks[k], r8)
    e_ref[...] = e8
    r_ref[...] = r8
    rowg = lax.broadcasted_iota(jnp.int32, (LANES, tm), 0)
    g_t = jnp.zeros((LANES, tm), F32)
    for k in range(TOP_K):
        g_t = jnp.where(rowg == k, gates[k], g_t)
    gt_ref[...] = g_t.T


def _post(x2, mod8, mod_off, attn_outs, z, conv_w, conv_b, ln_g, ln_b, w_out, g2, w_router2, b_router_t,
          cnt0, batch, seq):
    tm = POST_TM
    tokens = batch * seq
    tiles_per_row = seq // tm
    nh = tm // CONV_HALO
    last_h = tokens // CONV_HALO - 1
    row = lambda w: pl.BlockSpec((tm, w), lambda i: (i, 0))
    full = lambda a, b: pl.BlockSpec((a, b), lambda i: (0, 0))
    (o1, l1), (o4, l4), (o16, l16) = attn_outs
    d4, d16 = DILATIONS[1], DILATIONS[2]
    strided = lambda d, w: pl.BlockSpec((tm // d, d * w), lambda i: (i, 0))
    out_shapes = (jax.ShapeDtypeStruct((tokens, D_MODEL), F32),
                  jax.ShapeDtypeStruct((tokens, ROW_WORDS), jnp.uint32),
                  jax.ShapeDtypeStruct((8, tokens), jnp.int32),
                  jax.ShapeDtypeStruct((8, tokens), jnp.int32),
                  jax.ShapeDtypeStruct((tokens, LANES), F32),
                  jax.ShapeDtypeStruct((N_EXPERTS, LANES), F32))
    return pl.pallas_call(
        functools.partial(_post_kernel, tiles_per_row=tiles_per_row),
        out_shape=out_shapes,
        grid=(tokens // tm,),
        in_specs=[row(D_MODEL),
                  pl.BlockSpec((None, 8, D_MODEL), lambda i: (mod_off + i // tiles_per_row, 0, 0)),
                  row(ATTN_W), strided(d4, ATTN_W), strided(d16, ATTN_W),
                  row(LANES), strided(d4, LANES), strided(d16, LANES),
                  pl.BlockSpec((CONV_HALO, CONV_W), lambda i: (jnp.maximum(i * nh - 1, 0), 0)),
                  row(CONV_W),
                  pl.BlockSpec((CONV_HALO, CONV_W), lambda i: (jnp.minimum((i + 1) * nh, last_h), 0)),
                  full(32, CONV_W), full(1, CONV_W), full(1, CONV_W), full(1, CONV_W),
                  full(D_MODEL, D_MODEL), full(1, D_MODEL), full(D_MODEL, ROUTER_W),
                  full(N_EXPERTS, tm), full(N_EXPERTS, LANES)],
        out_specs=(row(D_MODEL), row(ROW_WORDS),
                   pl.BlockSpec((8, tm), lambda i: (0, i)), pl.BlockSpec((8, tm), lambda i: (0, i)),
                   row(LANES), full(N_EXPERTS, LANES)),
        scratch_shapes=[pltpu.VMEM((tm + 2 * CONV_HALO, CONV_W), F32),
                        pltpu.VMEM((SUBLANES, tm + 2 * CONV_HALO, CONV_W), F32),
                        pltpu.VMEM((tm, ATTN_W + CONV_W), BF16),
                        pltpu.VMEM((N_EXPERTS, LANES), F32),
                        pltpu.VMEM((2, ATTN_W // LANES, tm, LANES), F32),
                        pltpu.VMEM((2, tm, LANES), F32)],
        compiler_params=_cparams("arbitrary"),
        name="post",
    )(x2, mod8, o1, o4, o16, l1, l4, l16, z, z, z, conv_w, conv_b, ln_g, ln_b, w_out, g2,
      w_router2, b_router_t, cnt0)


MOE_BM = 256

SLOT_TL = 2048
SC_CORES = 2
SC_SUBCORES = 16
SC_WORKERS = SC_CORES * SC_SUBCORES
SC_CHUNK = 64


def _slot_kernel(plan_ref, e_ref, r_ref, s_ref):
    e = e_ref[...]
    start = jnp.zeros(e.shape, jnp.int32)
    for x in range(N_EXPERTS):
        start = jnp.where(e == x, plan_ref[x], start)
    s_ref[...] = start + r_ref[...]


def _slots(plan, e8, r8):
    tokens = e8.shape[1]
    spec = pl.BlockSpec((8, SLOT_TL), lambda i, ps: (0, i))
    s8 = pl.pallas_call(
        _slot_kernel,
        out_shape=jax.ShapeDtypeStruct((8, tokens), jnp.int32),
        grid_spec=pltpu.PrefetchScalarGridSpec(
            num_scalar_prefetch=1, grid=(tokens // SLOT_TL,), in_specs=[spec, spec], out_specs=spec),
        compiler_params=_cparams("arbitrary"),
        name="slots",
    )(plan, e8, r8)
    return s8[:TOP_K].reshape(TOP_K, tokens // SC_CHUNK, SC_CHUNK).transpose(1, 0, 2)


def _sc_worker_chunks(tokens):
    return tokens // (SC_WORKERS * SC_CHUNK)


def _sc_mesh():
    return plsc.VectorSubcoreMesh(core_axis_name="c", subcore_axis_name="s")


def _dispatch(slots, h2, n_pad):
    tokens = h2.shape[0]
    n_chunks = _sc_worker_chunks(tokens)

    @functools.partial(
        pl.kernel, mesh=_sc_mesh(),
        out_type=jax.ShapeDtypeStruct((n_pad, ROW_WORDS), jnp.uint32),
        scratch_types=[pltpu.VMEM((TOP_K, SC_CHUNK), jnp.int32),
                       pltpu.VMEM((SC_CHUNK, ROW_WORDS), jnp.uint32),
                       pltpu.SemaphoreType.DMA],
        name="sc_dispatch")
    def scatter(h_hbm, slot_hbm, rows_hbm, idx_v, rows_v, sem):
        worker = lax.axis_index("s") * SC_CORES + lax.axis_index("c")

        @pl.loop(0, n_chunks)
        def _(ci):
            chunk = worker * n_chunks + ci
            pltpu.sync_copy(slot_hbm.at[chunk], idx_v)
            pltpu.sync_copy(h_hbm.at[pl.ds(chunk * SC_CHUNK, SC_CHUNK)], rows_v)
            copies = [pltpu.async_copy(rows_v, rows_hbm.at[idx_v.at[k]], sem) for k in range(TOP_K)]
            for cp in copies:
                cp.wait()

    return scatter(h2, slots)


FF_CHUNK = 1024


CAST_ROWS = 64


ST_UNITS = 0
ST_PENDING = 1


def _experts_kernel(nblk_ref, bstart_ref, next_ref, x_hbm, wu_ref, bu_ref, wd_ref, bd_ref, y_hbm,
                    wub_ref, wdb_ref, xbuf_ref, ybuf_ref, state_ref, sem_in, sem_out):
    e = pl.program_id(0)
    n = nblk_ref[e]
    b0 = bstart_ref[e]
    n_full = n // 2
    odd = n % 2
    big = 2 * MOE_BM

    @pl.when(e == 0)
    def _():
        state_ref[ST_UNITS] = 0
        state_ref[ST_PENDING] = 0
        state_ref[ST_PENDING + 1] = 0

    def rows_at(ref, block, rows):
        return ref.at[pl.ds(pl.multiple_of(block * MOE_BM, MOE_BM), rows)]

    def x_copy(block, rows, slot):
        return pltpu.make_async_copy(rows_at(x_hbm, block, rows), xbuf_ref.at[slot, pl.ds(0, rows)],
                                     sem_in.at[slot])

    def y_copy(block, rows, slot):
        return pltpu.make_async_copy(ybuf_ref.at[slot, pl.ds(0, rows)], rows_at(y_hbm, block, rows),
                                     sem_out.at[slot])

    def start_load(block, blocks_left, slot):
        @pl.when(blocks_left >= 2)
        def _():
            x_copy(block, big, slot).start()

        @pl.when(blocks_left == 1)
        def _():
            x_copy(block, MOE_BM, slot).start()

    def start_next_load(u, slot):
        left = n - 2 * (u + 1)
        nxt = next_ref[e]
        nxt_c = jnp.maximum(nxt, 0)
        block = jnp.where(left > 0, b0 + 2 * (u + 1), bstart_ref[nxt_c])
        blocks_left = jnp.where(left > 0, left, jnp.where(nxt >= 0, nblk_ref[nxt_c], 0))
        start_load(block, blocks_left, slot)

    def wait_store(slot):
        pending = state_ref[ST_PENDING + slot]
        for rows in (big, MOE_BM):
            @pl.when(pending == rows)
            def _():
                y_copy(0, rows, slot).wait()

        state_ref[ST_PENDING + slot] = 0

    def run_unit(u, rows):
        slot = (state_ref[ST_UNITS] + u) % 2
        x_copy(0, rows, slot).wait()
        start_next_load(u, 1 - slot)
        wait_store(slot)
        compute(slot, rows)
        y_copy(b0 + 2 * u, rows, slot).start()
        state_ref[ST_PENDING + slot] = rows

    def compute(slot, rows):
        x_lo, x_hi = _unpack_halves(xbuf_ref[slot, pl.ds(0, rows), :])
        xb = jnp.concatenate([x_lo.astype(BF16), x_hi.astype(BF16)], axis=1)
        acc = jnp.broadcast_to(bd_ref[...], (rows, D_MODEL))
        for c in range(D_FF // FF_CHUNK):
            lo = c * FF_CHUNK
            glu = (jnp.dot(xb, wub_ref[:, lo:lo + FF_CHUNK], preferred_element_type=F32)
                   + bu_ref[:, lo:lo + FF_CHUNK])
            lin = (jnp.dot(xb, wub_ref[:, D_FF + lo:D_FF + lo + FF_CHUNK], preferred_element_type=F32)
                   + bu_ref[:, D_FF + lo:D_FF + lo + FF_CHUNK])
            glu = jnp.minimum(glu, SWIGLU_LIMIT)
            lin = jnp.clip(lin, -SWIGLU_LIMIT, SWIGLU_LIMIT)
            act = glu * jax.nn.sigmoid(SWIGLU_ALPHA * glu) * (lin + 1.0)
            acc = acc + jnp.dot(act.astype(BF16), wdb_ref[lo:lo + FF_CHUNK, :], preferred_element_type=F32)
        ybuf_ref[slot, pl.ds(0, rows), :] = _pack_halves(acc.astype(BF16).astype(F32))

    @pl.when(n > 0)
    def _():
        @pl.when(state_ref[ST_UNITS] == 0)
        def _():
            start_load(b0, n, 0)

        def cast_rows(i, carry):
            r = pl.multiple_of(i * CAST_ROWS, CAST_ROWS)
            wub_ref[pl.ds(r, CAST_ROWS), :] = wu_ref[pl.ds(r, CAST_ROWS), :].astype(BF16)
            wdb_ref[pl.ds(r, CAST_ROWS), :] = wd_ref[pl.ds(r, CAST_ROWS), :].astype(BF16)
            return carry

        lax.fori_loop(0, D_MODEL // CAST_ROWS, cast_rows, 0)

        def full_unit(u, carry):
            run_unit(u, big)
            return carry

        lax.fori_loop(0, n_full, full_unit, 0)

        @pl.when(odd == 1)
        def _():
            run_unit(n_full, MOE_BM)

        state_ref[ST_UNITS] = state_ref[ST_UNITS] + n_full + odd

    @pl.when(e == N_EXPERTS - 1)
    def _():
        wait_store(0)
        wait_store(1)


def _experts(nblk, bstart, rows, w_up, b_up, w_down, b_down):
    n_pad = rows.shape[0]
    per_expert = lambda *tail: pl.BlockSpec((None, *tail), lambda e, nb, bs, nx: (e, 0, 0))
    ids = jnp.arange(N_EXPERTS, dtype=jnp.int32)
    later = (ids[None, :] > ids[:, None]) & (nblk[None, :] > 0)
    nxt = jnp.where(later.any(axis=1), jnp.argmax(later, axis=1), -1).astype(jnp.int32)
    return pl.pallas_call(
        _experts_kernel,
        out_shape=jax.ShapeDtypeStruct((n_pad, ROW_WORDS), jnp.uint32),
        grid_spec=pltpu.PrefetchScalarGridSpec(
            num_scalar_prefetch=3, grid=(N_EXPERTS,),
            in_specs=[pl.BlockSpec(memory_space=pl.ANY),
                      per_expert(D_MODEL, 2 * D_FF), per_expert(1, 2 * D_FF),
                      per_expert(D_FF, D_MODEL), per_expert(1, D_MODEL)],
            out_specs=pl.BlockSpec(memory_space=pl.ANY),
            scratch_shapes=[pltpu.VMEM((D_MODEL, 2 * D_FF), BF16), pltpu.VMEM((D_FF, D_MODEL), BF16),
                            pltpu.VMEM((2, 2 * MOE_BM, ROW_WORDS), jnp.uint32),
                            pltpu.VMEM((2, 2 * MOE_BM, ROW_WORDS), jnp.uint32),
                            pltpu.SMEM((3,), jnp.int32),
                            pltpu.SemaphoreType.DMA((2,)), pltpu.SemaphoreType.DMA((2,))]),
        compiler_params=_cparams("arbitrary"),
        name="experts",
    )(nblk, bstart, nxt, rows, w_up, b_up, w_down, b_down)


COMBINE_TM = 256


def _gather_rows(slots, y_rows, tokens):
    n_chunks = _sc_worker_chunks(tokens)

    @functools.partial(
        pl.kernel, mesh=_sc_mesh(),
        out_type=jax.ShapeDtypeStruct((TOP_K, tokens, ROW_WORDS), jnp.uint32),
        scratch_types=[pltpu.VMEM((TOP_K, SC_CHUNK), jnp.int32),
                       pltpu.VMEM((SC_CHUNK, ROW_WORDS), jnp.uint32),
                       pltpu.SemaphoreType.DMA],
        name="sc_gather")
    def gather(y_hbm, slot_hbm, out_hbm, idx_v, rows_v, sem):
        worker = lax.axis_index("s") * SC_CORES + lax.axis_index("c")

        @pl.loop(0, n_chunks)
        def _(ci):
            chunk = worker * n_chunks + ci
            pltpu.sync_copy(slot_hbm.at[chunk], idx_v)
            for k in range(TOP_K):
                pltpu.async_copy(y_hbm.at[idx_v.at[k]], rows_v, sem).wait()
                pltpu.sync_copy(rows_v, out_hbm.at[k, pl.ds(chunk * SC_CHUNK, SC_CHUNK)])

    return gather(y_rows, slots)


def _combine_kernel(xn_ref, gt_ref, mod_ref, fg_ref, rows_ref, out_ref):
    gt = gt_ref[...]
    y_lo = y_hi = None
    for k in range(TOP_K):
        lo, hi = _unpack_halves(rows_ref[k])
        g = gt[:, k:k + 1]
        y_lo = g * lo if y_lo is None else y_lo + g * lo
        y_hi = g * hi if y_hi is None else y_hi + g * hi
    y = jnp.concatenate([y_lo, y_hi], axis=1)
    xo = xn_ref[...] + mod_ref[5:6, :] * y
    out_ref[...] = _rmsnorm_rows(xo, fg_ref[...])


def _combine(xn, gt, mod8, mod_off, final_g, gathered, batch, seq):
    tm = COMBINE_TM
    tokens = batch * seq
    tiles_per_row = seq // tm
    return pl.pallas_call(
        _combine_kernel,
        out_shape=jax.ShapeDtypeStruct((tokens, D_MODEL), F32),
        grid=(tokens // tm,),
        in_specs=[pl.BlockSpec((tm, D_MODEL), lambda i: (i, 0)),
                  pl.BlockSpec((tm, LANES), lambda i: (i, 0)),
                  pl.BlockSpec((None, 8, D_MODEL), lambda i: (mod_off + i // tiles_per_row, 0, 0)),
                  pl.BlockSpec((1, D_MODEL), lambda i: (0, 0)),
                  pl.BlockSpec((TOP_K, tm, ROW_WORDS), lambda i: (0, i, 0))],
        out_specs=pl.BlockSpec((tm, D_MODEL), lambda i: (i, 0)),
        compiler_params=_cparams("arbitrary"),
        name="combine",
    )(xn, gt, mod8, final_g, gathered)


def _block_plan(counts):
    nblk = (counts + MOE_BM - 1) // MOE_BM
    bstart = jnp.cumsum(nblk) - nblk
    return (bstart * MOE_BM).astype(jnp.int32), nblk.astype(jnp.int32), bstart.astype(jnp.int32)


def _run_group(x, mod8, mod_off, p):
    batch, seq, _ = x.shape
    tokens = batch * seq
    x2 = x.reshape(tokens, D_MODEL)
    qkv, z = _in_proj(x2, mod8, mod_off, p["g1"], p["w_in"], _rope_tables(seq), batch, seq, tm=512)
    attn_outs = [_attention_pattern(*qkv_d, batch, seq, d) for qkv_d, d in zip(qkv, DILATIONS)]
    cnt0 = jnp.zeros((N_EXPERTS, LANES), F32)
    xn, h2, e8, r8, gt, cnt = _post(x2, mod8, mod_off, attn_outs, z, p["conv_w"], p["conv_b"], p["ln_g"],
                                    p["ln_b"], p["w_out"], p["g2"], p["w_router2"], p["b_router_t"],
                                    cnt0, batch, seq)
    n_pad = tokens * TOP_K + N_EXPERTS * MOE_BM
    pstart, nblk, bstart = _block_plan(cnt[:, 0].astype(jnp.int32))
    slots = _slots(pstart, e8, r8)
    rows = _dispatch(slots, h2, n_pad)
    y_rows = _experts(nblk, bstart, rows, p["w_up"], p["b_up"], p["w_down"], p["b_down"])
    gathered = _gather_rows(slots, y_rows, tokens)
    out = _combine(xn, gt, mod8, mod_off, p["final_g"], gathered, batch, seq)
    return out.reshape(batch, seq, D_MODEL)


def _prepare(w_ada, b_ada, norm_mix_g, w_in, conv_w, conv_b, conv_ln_g, conv_ln_b, w_out, norm_ffn_g,
             w_router, b_router, w_up, b_up, w_down, b_down, final_g):
    wr = w_router[0]
    wr_hi = wr.astype(BF16)
    wr_lo = (wr - wr_hi.astype(F32)).astype(BF16)
    w_router2 = jnp.concatenate(
        [wr_hi, wr_lo, jnp.zeros((D_MODEL, ROUTER_W - 2 * N_EXPERTS), BF16)], axis=1)
    return dict(
        g1=norm_mix_g[0][None, :], w_in=w_in[0].astype(BF16),
        conv_w=jnp.concatenate([conv_w[0], jnp.zeros((32 - CONV_K, CONV_W), F32)], axis=0),
        conv_b=conv_b[0][None, :], ln_g=conv_ln_g[0][None, :], ln_b=conv_ln_b[0][None, :],
        w_out=w_out[0].astype(BF16), g2=norm_ffn_g[0][None, :],
        w_router2=w_router2,
        b_router_t=jnp.broadcast_to(b_router[0][:, None], (N_EXPERTS, POST_TM)),
        w_up=w_up[0], b_up=b_up[0][:, None, :],
        w_down=w_down[0], b_down=b_down[0][:, None, :],
        final_g=final_g[None, :])


def kernel(x_prompt, x_sample, c_prompt, c_sample, w_ada, b_ada, norm_mix_g, w_in, conv_w, conv_b,
           conv_ln_g, conv_ln_b, w_out, norm_ffn_g, w_router, b_router, w_up, b_up, w_down, b_down,
           final_g):
    p = _prepare(w_ada, b_ada, norm_mix_g, w_in, conv_w, conv_b, conv_ln_g, conv_ln_b, w_out,
                 norm_ffn_g, w_router, b_router, w_up, b_up, w_down, b_down, final_g)
    nb_p, nb_s = c_prompt.shape[0], c_sample.shape[0]
    c_rows = 16
    c_all = jnp.concatenate([c_prompt, c_sample, jnp.zeros((c_rows - nb_p - nb_s, D_MODEL), F32)], axis=0)
    mod = _modulation(c_all, w_ada[0].astype(BF16), b_ada[0][None, :])
    mod8 = jnp.concatenate([mod.reshape(c_rows, 6, D_MODEL), jnp.zeros((c_rows, 2, D_MODEL), F32)], axis=1)
    y_prompt = _run_group(x_prompt, mod8, 0, p)
    y_sample = _run_group(x_sample, mod8, nb_p, p)
    return (y_prompt, y_sample)
```

```python
import functools

import jax
import jax.numpy as jnp
from jax import lax
from jax.experimental import pallas as pl
from jax.experimental.pallas import tpu as pltpu
from jax.experimental.pallas import tpu_sc as plsc

D_MODEL = 1024
N_HEADS = 8
HEAD_DIM = 64
ATTN_W = N_HEADS * HEAD_DIM
CONV_W = D_MODEL // 2
IN_PROJ_W = 3 * ATTN_W + 2 * CONV_W
WINDOWS = (128, 512, 2048)
DILATIONS = (1, 4, 16)
ATTN_BLOCK = 128
HALF = 64
ROPE_DIM = HEAD_DIM // 4
ROPE_THETA = 500000.0
NEG_INF = -1e30
CONV_K = 31
CONV_PAD = CONV_K // 2
N_EXPERTS = 32
TOP_K = 4
D_FF = D_MODEL
SWIGLU_ALPHA = 1.702
SWIGLU_LIMIT = 7.0
EPS = 1e-6

LANES = 128
SUBLANES = 8
VMEM_LIMIT = 56 * 1024 * 1024

F32 = jnp.float32
BF16 = jnp.bfloat16


def _cparams(*sem):
    return pltpu.CompilerParams(dimension_semantics=sem, vmem_limit_bytes=VMEM_LIMIT)


def _mod_kernel(c_ref, w_ref, b_ref, o_ref):
    c = c_ref[...]
    a = (c * jax.nn.sigmoid(c)).astype(BF16)
    o_ref[...] = jnp.dot(a, w_ref[...], preferred_element_type=F32) + b_ref[...]


def _modulation(c_all, w_ada, b_ada):
    rows = c_all.shape[0]
    n = w_ada.shape[1]
    tn = 1536
    return pl.pallas_call(
        _mod_kernel,
        out_shape=jax.ShapeDtypeStruct((rows, n), F32),
        grid=(n // tn,),
        in_specs=[pl.BlockSpec((rows, D_MODEL), lambda j: (0, 0)),
                  pl.BlockSpec((D_MODEL, tn), lambda j: (0, j)),
                  pl.BlockSpec((1, tn), lambda j: (0, j))],
        out_specs=pl.BlockSpec((rows, tn), lambda j: (0, j)),
        compiler_params=_cparams("arbitrary"),
        name="modulation",
    )(c_all, w_ada, b_ada)


def _rmsnorm_rows(x, g):
    return x * lax.rsqrt(jnp.mean(x * x, axis=-1, keepdims=True) + EPS) * g


def _pack_halves(x):
    n = x.shape[1] // 2
    lo = lax.bitcast_convert_type(x[:, :n], jnp.uint32)
    hi = lax.bitcast_convert_type(x[:, n:], jnp.uint32)
    return (lo >> 16) | (hi & jnp.uint32(0xFFFF0000))


def _unpack_halves(p):
    lo = lax.bitcast_convert_type(p << 16, F32)
    hi = lax.bitcast_convert_type(p & jnp.uint32(0xFFFF0000), F32)
    return lo, hi


def _inproj_kernel(x_ref, mod_ref, g_ref, w_ref, cos_ref, s1_ref, s2_ref,
                   q1_ref, k1_ref, v1_ref, q4_ref, k4_ref, v4_ref, q16_ref, k16_ref, v16_ref,
                   z_ref, stage_ref):
    tm = x_ref.shape[0]
    x = x_ref[...]
    shift = mod_ref[0:1, :]
    scale = mod_ref[1:2, :]
    h = _rmsnorm_rows(x, g_ref[...]) * (1.0 + scale) + shift
    hb = h.astype(BF16)
    cos = cos_ref[...]
    s1 = s1_ref[...]
    s2 = s2_ref[...]

    def rope(p):
        return p * cos + pltpu.roll(p, LANES - ROPE_DIM // 2, 1) * s1 + pltpu.roll(p, ROPE_DIM // 2, 1) * s2

    n_chunks = ATTN_W // LANES
    plans = ((0, lambda p: rope(p) * (HEAD_DIM ** -0.5), (q1_ref, q4_ref, q16_ref)),
             (ATTN_W, rope, (k1_ref, k4_ref, k16_ref)),
             (2 * ATTN_W, lambda p: p, (v1_ref, v4_ref, v16_ref)))
    for col0, finish, (nat_ref, *strided_refs) in plans:
        proj = jnp.dot(hb, w_ref[:, col0:col0 + ATTN_W], preferred_element_type=F32)
        for c in range(n_chunks):
            lo = c * LANES
            val = finish(proj[:, lo:lo + LANES])
            nat_ref[:, lo:lo + LANES] = val.astype(BF16)
            stage_ref[c] = val
        for d, out_ref in zip(DILATIONS[1:], strided_refs):
            for r in range(d):
                for c in range(n_chunks):
                    lo = r * ATTN_W + c * LANES
                    out_ref[:, lo:lo + LANES] = stage_ref[c, pl.ds(r, tm // d, stride=d), :].astype(BF16)
    a = jnp.dot(hb, w_ref[:, 3 * ATTN_W:3 * ATTN_W + CONV_W], preferred_element_type=F32)
    g = jnp.dot(hb, w_ref[:, 3 * ATTN_W + CONV_W:], preferred_element_type=F32)
    z_ref[...] = (a * jax.nn.sigmoid(g)).astype(BF16)


def _rope_tables(seq):
    half = ROPE_DIM // 2
    inv_freq = ROPE_THETA ** (-jnp.arange(half, dtype=F32) * 2.0 / ROPE_DIM)
    ang = jnp.arange(seq, dtype=F32)[:, None] * inv_freq[None, :]
    cos = jnp.cos(ang)
    sin = jnp.sin(ang)
    ones = jnp.ones((seq, HEAD_DIM - ROPE_DIM), F32)
    zeros = jnp.zeros((seq, HEAD_DIM - ROPE_DIM), F32)
    zh = jnp.zeros((seq, half), F32)
    cos_h = jnp.concatenate([cos, cos, ones], axis=1)
    s1_h = jnp.concatenate([-sin, zh, zeros], axis=1)
    s2_h = jnp.concatenate([zh, sin, zeros], axis=1)
    rep = LANES // HEAD_DIM
    return tuple(jnp.tile(t, (1, rep)) for t in (cos_h, s1_h, s2_h))


def _in_proj(x2, mod8, mod_off, g1, w_in, tables, batch, seq, tm):
    tokens = batch * seq
    tiles_per_row = seq // tm
    row_spec = lambda w: pl.BlockSpec((tm, w), lambda i: (i, 0))
    tab_spec = pl.BlockSpec((tm, LANES), lambda i: (i % tiles_per_row, 0))
    qkv_shapes, qkv_specs = [], []
    for d in DILATIONS:
        qkv_shapes += [jax.ShapeDtypeStruct((tokens // d, d * ATTN_W), BF16)] * 3
        qkv_specs += [pl.BlockSpec((tm // d, d * ATTN_W), lambda i: (i, 0))] * 3
    outs = pl.pallas_call(
        _inproj_kernel,
        out_shape=(*qkv_shapes, jax.ShapeDtypeStruct((tokens, CONV_W), BF16)),
        grid=(tokens // tm,),
        in_specs=[row_spec(D_MODEL),
                  pl.BlockSpec((None, 8, D_MODEL), lambda i: (mod_off + i // tiles_per_row, 0, 0)),
                  pl.BlockSpec((1, D_MODEL), lambda i: (0, 0)),
                  pl.BlockSpec((D_MODEL, IN_PROJ_W), lambda i: (0, 0)),
                  tab_spec, tab_spec, tab_spec],
        out_specs=(*qkv_specs, row_spec(CONV_W)),
        scratch_shapes=[pltpu.VMEM((ATTN_W // LANES, tm, LANES), F32)],
        compiler_params=_cparams("arbitrary"),
        name="in_proj",
    )(x2, mod8, g1, w_in, *tables)
    qkv = [outs[3 * n:3 * n + 3] for n in range(len(DILATIONS))]
    return qkv, outs[-1]


def _attn_kernel(q_ref, kp_ref, kc_ref, kn_ref, vp_ref, vc_ref, vn_ref, o_ref, lse_ref,
                 kw_ref, vw_ref, *, tq, length, nres):
    j = pl.program_id(2)
    kw_ref[0:HALF, :] = kp_ref[...]
    kw_ref[HALF:HALF + tq, :] = kc_ref[...]
    kw_ref[HALF + tq:, :] = kn_ref[...]
    vw_ref[0:HALF, :] = vp_ref[...]
    vw_ref[HALF:HALF + tq, :] = vc_ref[...]
    vw_ref[HALF + tq:, :] = vn_ref[...]

    nk = 2 * ATTN_BLOCK
    t_io = lax.broadcasted_iota(jnp.int32, (ATTN_BLOCK, nk), 0)
    u_io = lax.broadcasted_iota(jnp.int32, (ATTN_BLOCK, nk), 1)
    lane = lax.broadcasted_iota(jnp.int32, (ATTN_BLOCK, LANES), 1)
    first_head = lane < HEAD_DIM
    lane1 = lax.broadcasted_iota(jnp.int32, (1, LANES), 1)
    head_keep = [jnp.where(lane1 < HEAD_DIM, 1.0, 0.0).astype(BF16),
                 jnp.where(lane1 < HEAD_DIM, 0.0, 1.0).astype(BF16)]

    for blk in range(tq // ATTN_BLOCK):
        r0 = blk * ATTN_BLOCK
        base = j * tq + r0
        u_min = jnp.maximum(t_io, HALF - base)
        u_max = jnp.minimum(t_io + 2 * HALF, length + HALF - 1 - base)
        mask = (u_io >= u_min) & (u_io <= u_max)
        for res in range(nres):
            lse_tile = jnp.zeros((ATTN_BLOCK, LANES), F32)
            for c in range(ATTN_W // LANES):
                lo = res * ATTN_W + c * LANES
                qc = q_ref[r0:r0 + ATTN_BLOCK, lo:lo + LANES]
                kc = kw_ref[r0:r0 + nk, lo:lo + LANES]
                vc = vw_ref[r0:r0 + nk, lo:lo + LANES]
                outs = []
                for hh in range(LANES // HEAD_DIM):
                    qm = qc * head_keep[hh]
                    s = lax.dot_general(qm, kc, (((1,), (1,)), ((), ())), preferred_element_type=F32)
                    s = jnp.where(mask, s, NEG_INF)
                    m = jnp.max(s, axis=-1, keepdims=True)
                    p = jnp.exp(s - m)
                    den = jnp.sum(p, axis=-1, keepdims=True)
                    pv = jnp.dot(p.astype(BF16), vc, preferred_element_type=F32)
                    outs.append(pv / den)
                    lse_tile = jnp.where(lane == c * (LANES // HEAD_DIM) + hh, m + jnp.log(den), lse_tile)
                o_ref[r0:r0 + ATTN_BLOCK, lo:lo + LANES] = jnp.where(first_head, outs[0], outs[1]).astype(BF16)
            lse_ref[r0:r0 + ATTN_BLOCK, res * LANES:(res + 1) * LANES] = lse_tile


ATTN_QUERIES = 2048


def _attention_pattern(q, k, v, batch, seq, d):
    length = seq // d
    tq = min(length, ATTN_QUERIES)
    nres = min(d, ATTN_QUERIES // tq)
    view = lambda t: t.reshape(batch, length, d * ATTN_W)
    nh = tq // HALF
    last_h = length // HALF - 1
    width = nres * ATTN_W
    cur = pl.BlockSpec((None, tq, width), lambda b, r, j: (b, j, r))
    prev = pl.BlockSpec((None, HALF, width), lambda b, r, j: (b, jnp.maximum(j * nh - 1, 0), r))
    nxt = pl.BlockSpec((None, HALF, width), lambda b, r, j: (b, jnp.minimum((j + 1) * nh, last_h), r))
    o, lse = pl.pallas_call(
        functools.partial(_attn_kernel, tq=tq, length=length, nres=nres),
        out_shape=(jax.ShapeDtypeStruct((batch, length, d * ATTN_W), BF16),
                   jax.ShapeDtypeStruct((batch, length, d * LANES), F32)),
        grid=(batch, d // nres, length // tq),
        in_specs=[cur, prev, cur, nxt, prev, cur, nxt],
        out_specs=(cur, pl.BlockSpec((None, tq, nres * LANES), lambda b, r, j: (b, j, r))),
        scratch_shapes=[pltpu.VMEM((tq + 2 * HALF, width), BF16),
                        pltpu.VMEM((tq + 2 * HALF, width), BF16)],
        compiler_params=_cparams("arbitrary", "arbitrary", "arbitrary"),
        name=f"attention_d{d}",
    )(view(q), view(k), view(k), view(k), view(v), view(v), view(v))
    rows = batch * length
    return o.reshape(rows, d * ATTN_W), lse.reshape(rows, d * LANES)


POST_TM = 256
CONV_HALO = 16
CONV_ROWS = 64
ROUTER_W = 128
ROW_WORDS = D_MODEL // 2


def _post_kernel(x_ref, mod_ref, o1_ref, o4_ref, o16_ref, l1_ref, l4_ref, l16_ref,
                 zp_ref, zc_ref, zn_ref, cw_ref, cb_ref, lng_ref, lnb_ref,
                 wo_ref, g2_ref, wr_ref, br_ref, cnt0_ref,
                 xn_ref, h2_ref, e_ref, r_ref, gt_ref, cnt_ref,
                 zw_ref, zs_ref, cat_ref, carry_ref, on_ref, ln_ref, *, tiles_per_row):
    tm = POST_TM
    i = pl.program_id(0)

    @pl.when(i == 0)
    def _():
        carry_ref[...] = cnt0_ref[...]

    pos = i % tiles_per_row
    zp = zp_ref[...].astype(F32)
    zn = zn_ref[...].astype(F32)
    zw_ref[0:CONV_HALO, :] = jnp.where(pos == 0, 0.0, zp)
    zw_ref[CONV_HALO:CONV_HALO + tm, :] = zc_ref[...].astype(F32)
    zw_ref[CONV_HALO + tm:, :] = jnp.where(pos == tiles_per_row - 1, 0.0, zn)
    cw = cw_ref[...]
    off = CONV_HALO - CONV_PAD
    span = tm + 2 * CONV_HALO - SUBLANES
    for s in range(SUBLANES):
        zs_ref[s, 0:span, :] = zw_ref[s:s + span, :]
    for rc in range(tm // CONV_ROWS):
        r0 = rc * CONV_ROWS
        acc = jnp.broadcast_to(cb_ref[...], (CONV_ROWS, CONV_W))
        for tap in range(CONV_K):
            q, s = divmod(off + tap, SUBLANES)
            lo = r0 + q * SUBLANES
            acc = acc + zs_ref[s, lo:lo + CONV_ROWS, :] * cw[tap:tap + 1, :]
        mu = jnp.mean(acc, axis=-1, keepdims=True)
        cen = acc - mu
        var = jnp.mean(cen * cen, axis=-1, keepdims=True)
        zf = cen * lax.rsqrt(var + EPS) * lng_ref[...] + lnb_ref[...]
        cat_ref[r0:r0 + CONV_ROWS, ATTN_W:] = (zf * jax.nn.sigmoid(zf)).astype(BF16)

    n_chunks = ATTN_W // LANES
    for pi, (d, o_ref, l_ref) in enumerate(((DILATIONS[1], o4_ref, l4_ref), (DILATIONS[2], o16_ref, l16_ref))):
        rows = tm // d
        for r in range(d):
            ln_ref[pi, pl.ds(r, rows, stride=d), :] = l_ref[:, r * LANES:(r + 1) * LANES]
            for c in range(n_chunks):
                lo = r * ATTN_W + c * LANES
                on_ref[pi, c, pl.ds(r, rows, stride=d), :] = o_ref[:, lo:lo + LANES].astype(F32)

    la, lb, lc = l1_ref[...], ln_ref[0], ln_ref[1]
    lmax = jnp.maximum(jnp.maximum(la, lb), lc)
    wa, wb, wc = jnp.exp(la - lmax), jnp.exp(lb - lmax), jnp.exp(lc - lmax)
    inv = 1.0 / (wa + wb + wc)
    wa, wb, wc = wa * inv, wb * inv, wc * inv
    first_head = lax.broadcasted_iota(jnp.int32, (tm, LANES), 1) < HEAD_DIM
    per_chunk = LANES // HEAD_DIM
    for c in range(n_chunks):
        h0 = c * per_chunk
        spread = lambda w: jnp.where(first_head, w[:, h0:h0 + 1], w[:, h0 + 1:h0 + 2])
        lo = c * LANES
        attn = (spread(wa) * o1_ref[:, lo:lo + LANES].astype(F32)
                + spread(wb) * on_ref[0, c] + spread(wc) * on_ref[1, c])
        cat_ref[:, lo:lo + LANES] = attn.astype(BF16)

    mix = jnp.dot(cat_ref[...], wo_ref[...], preferred_element_type=F32)
    gate1 = mod_ref[2:3, :]
    xn = x_ref[...] + gate1 * mix
    xn_ref[...] = xn
    h2 = _rmsnorm_rows(xn, g2_ref[...]) * (1.0 + mod_ref[4:5, :]) + mod_ref[3:4, :]

    hi = h2.astype(BF16)
    hi_f = hi.astype(F32)
    h2_ref[...] = _pack_halves(hi_f)
    lo = (h2 - hi_f).astype(BF16)
    wr = wr_ref[...]
    both = jnp.dot(hi, wr, preferred_element_type=F32) + jnp.dot(lo, wr, preferred_element_type=F32)
    bt = both.T
    logit = bt[0:N_EXPERTS, :] + bt[N_EXPERTS:2 * N_EXPERTS, :] + br_ref[...]

    row = lax.broadcasted_iota(jnp.int32, (N_EXPERTS, tm), 0).astype(F32)
    work = logit
    sel = jnp.zeros((N_EXPERTS, tm), F32)
    idxs, vals, hots = [], [], []
    for _ in range(TOP_K):
        m = jnp.max(work, axis=0, keepdims=True)
        idx = jnp.min(jnp.where(work == m, row, float(N_EXPERTS)), axis=0, keepdims=True)
        hot = row == idx
        work = jnp.where(hot, -jnp.inf, work)
        sel = jnp.where(hot, 1.0, sel)
        idxs.append(idx.astype(jnp.int32))
        vals.append(m)
        hots.append(hot)
    exps = [jnp.exp(v - vals[0]) for v in vals]
    esum = exps[0] + exps[1] + exps[2] + exps[3]
    gates = [e / esum for e in exps]

    tr = lax.broadcasted_iota(jnp.int32, (tm, tm), 0)
    tc = lax.broadcasted_iota(jnp.int32, (tm, tm), 1)
    before = jnp.where(tr < tc, 1.0, 0.0).astype(BF16)
    cnt = jnp.dot(sel.astype(BF16), before, preferred_element_type=F32) + carry_ref[:, 0:1]
    ranks = [jnp.sum(jnp.where(h, cnt, 0.0), axis=0, keepdims=True).astype(jnp.int32) for h in hots]
    carry_ref[...] = carry_ref[...] + jnp.sum(sel, axis=1, keepdims=True)
    cnt_ref[...] = carry_ref[...]

    row8 = lax.broadcasted_iota(jnp.int32, (8, tm), 0)
    e8 = jnp.zeros((8, tm), jnp.int32)
    r8 = jnp.zeros((8, tm), jnp.int32)
    for k in range(TOP_K):
        e8 = jnp.where(row8 == k, idxs[k], e8)
        r8 = jnp.where(row8 == k, ranks[k], r8)
    e_ref[...] = e8
    r_ref[...] = r8
    rowg = lax.broadcasted_iota(jnp.int32, (LANES, tm), 0)
    g_t = jnp.zeros((LANES, tm), F32)
    for k in range(TOP_K):
        g_t = jnp.where(rowg == k, gates[k], g_t)
    gt_ref[...] = g_t.T


def _post(x2, mod8, mod_off, attn_outs, z, conv_w, conv_b, ln_g, ln_b, w_out, g2, w_router2, b_router_t,
          cnt0, batch, seq):
    tm = POST_TM
    tokens = batch * seq
    tiles_per_row = seq // tm
    nh = tm // CONV_HALO
    last_h = tokens // CONV_HALO - 1
    row = lambda w: pl.BlockSpec((tm, w), lambda i: (i, 0))
    full = lambda a, b: pl.BlockSpec((a, b), lambda i: (0, 0))
    (o1, l1), (o4, l4), (o16, l16) = attn_outs
    d4, d16 = DILATIONS[1], DILATIONS[2]
    strided = lambda d, w: pl.BlockSpec((tm // d, d * w), lambda i: (i, 0))
    out_shapes = (jax.ShapeDtypeStruct((tokens, D_MODEL), F32),
                  jax.ShapeDtypeStruct((tokens, ROW_WORDS), jnp.uint32),
                  jax.ShapeDtypeStruct((8, tokens), jnp.int32),
                  jax.ShapeDtypeStruct((8, tokens), jnp.int32),
                  jax.ShapeDtypeStruct((tokens, LANES), F32),
                  jax.ShapeDtypeStruct((N_EXPERTS, LANES), F32))
    return pl.pallas_call(
        functools.partial(_post_kernel, tiles_per_row=tiles_per_row),
        out_shape=out_shapes,
        grid=(tokens // tm,),
        in_specs=[row(D_MODEL),
                  pl.BlockSpec((None, 8, D_MODEL), lambda i: (mod_off + i // tiles_per_row, 0, 0)),
                  row(ATTN_W), strided(d4, ATTN_W), strided(d16, ATTN_W),
                  row(LANES), strided(d4, LANES), strided(d16, LANES),
                  pl.BlockSpec((CONV_HALO, CONV_W), lambda i: (jnp.maximum(i * nh - 1, 0), 0)),
                  row(CONV_W),
                  pl.BlockSpec((CONV_HALO, CONV_W), lambda i: (jnp.minimum((i + 1) * nh, last_h), 0)),
                  full(32, CONV_W), full(1, CONV_W), full(1, CONV_W), full(1, CONV_W),
                  full(D_MODEL, D_MODEL), full(1, D_MODEL), full(D_MODEL, ROUTER_W),
                  full(N_EXPERTS, tm), full(N_EXPERTS, LANES)],
        out_specs=(row(D_MODEL), row(ROW_WORDS),
                   pl.BlockSpec((8, tm), lambda i: (0, i)), pl.BlockSpec((8, tm), lambda i: (0, i)),
                   row(LANES), full(N_EXPERTS, LANES)),
        scratch_shapes=[pltpu.VMEM((tm + 2 * CONV_HALO, CONV_W), F32),
                        pltpu.VMEM((SUBLANES, tm + 2 * CONV_HALO, CONV_W), F32),
                        pltpu.VMEM((tm, ATTN_W + CONV_W), BF16),
                        pltpu.VMEM((N_EXPERTS, LANES), F32),
                        pltpu.VMEM((2, ATTN_W // LANES, tm, LANES), F32),
                        pltpu.VMEM((2, tm, LANES), F32)],
        compiler_params=_cparams("arbitrary"),
        name="post",
    )(x2, mod8, o1, o4, o16, l1, l4, l16, z, z, z, conv_w, conv_b, ln_g, ln_b, w_out, g2,
      w_router2, b_router_t, cnt0)


MOE_BM = 256

SLOT_TL = 2048
SC_CORES = 2
SC_SUBCORES = 16
SC_WORKERS = SC_CORES * SC_SUBCORES
SC_CHUNK = 64


def _slot_kernel(plan_ref, e_ref, r_ref, s_ref):
    e = e_ref[...]
    start = jnp.zeros(e.shape, jnp.int32)
    for x in range(N_EXPERTS):
        start = jnp.where(e == x, plan_ref[x], start)
    s_ref[...] = start + r_ref[...]


def _slots(plan, e8, r8):
    tokens = e8.shape[1]
    spec = pl.BlockSpec((8, SLOT_TL), lambda i, ps: (0, i))
    s8 = pl.pallas_call(
        _slot_kernel,
        out_shape=jax.ShapeDtypeStruct((8, tokens), jnp.int32),
        grid_spec=pltpu.PrefetchScalarGridSpec(
            num_scalar_prefetch=1, grid=(tokens // SLOT_TL,), in_specs=[spec, spec], out_specs=spec),
        compiler_params=_cparams("arbitrary"),
        name="slots",
    )(plan, e8, r8)
    return s8[:TOP_K].reshape(TOP_K, tokens // SC_CHUNK, SC_CHUNK).transpose(1, 0, 2)


def _sc_worker_chunks(tokens):
    return tokens // (SC_WORKERS * SC_CHUNK)


def _sc_mesh():
    return plsc.VectorSubcoreMesh(core_axis_name="c", subcore_axis_name="s")


def _dispatch(slots, h2, n_pad):
    tokens = h2.shape[0]
    n_chunks = _sc_worker_chunks(tokens)

    @functools.partial(
        pl.kernel, mesh=_sc_mesh(),
        out_type=jax.ShapeDtypeStruct((n_pad, ROW_WORDS), jnp.uint32),
        scratch_types=[pltpu.VMEM((TOP_K, SC_CHUNK), jnp.int32),
                       pltpu.VMEM((SC_CHUNK, ROW_WORDS), jnp.uint32),
                       pltpu.SemaphoreType.DMA],
        name="sc_dispatch")
    def scatter(h_hbm, slot_hbm, rows_hbm, idx_v, rows_v, sem):
        worker = lax.axis_index("s") * SC_CORES + lax.axis_index("c")

        @pl.loop(0, n_chunks)
        def _(ci):
            chunk = worker * n_chunks + ci
            pltpu.sync_copy(slot_hbm.at[chunk], idx_v)
            pltpu.sync_copy(h_hbm.at[pl.ds(chunk * SC_CHUNK, SC_CHUNK)], rows_v)
            copies = [pltpu.async_copy(rows_v, rows_hbm.at[idx_v.at[k]], sem) for k in range(TOP_K)]
            for cp in copies:
                cp.wait()

    return scatter(h2, slots)


FF_CHUNK = D_FF
CAST_ROWS = 64
ST_UNITS = 0
ST_PENDING = 1


def _experts_kernel(nblk_ref, bstart_ref, next_ref, x_hbm, wu_ref, bu_ref, wd_ref, bd_ref, y_hbm,
                    wub_ref, wdb_ref, xbuf_ref, ybuf_ref, state_ref, sem_in, sem_out):
    e = pl.program_id(0)
    n = nblk_ref[e]
    b0 = bstart_ref[e]
    n_full = n // 2
    odd = n % 2
    big = 2 * MOE_BM

    @pl.when(e == 0)
    def _():
        state_ref[ST_UNITS] = 0
        state_ref[ST_PENDING] = 0
        state_ref[ST_PENDING + 1] = 0

    def rows_at(ref, block, rows):
        return ref.at[pl.ds(pl.multiple_of(block * MOE_BM, MOE_BM), rows)]

    def x_copy(block, rows, slot):
        return pltpu.make_async_copy(rows_at(x_hbm, block, rows), xbuf_ref.at[slot, pl.ds(0, rows)],
                                     sem_in.at[slot])

    def y_copy(block, rows, slot):
        return pltpu.make_async_copy(ybuf_ref.at[slot, pl.ds(0, rows)], rows_at(y_hbm, block, rows),
                                     sem_out.at[slot])

    def start_load(block, blocks_left, slot):
        @pl.when(blocks_left >= 2)
        def _():
            x_copy(block, big, slot).start()

        @pl.when(blocks_left == 1)
        def _():
            x_copy(block, MOE_BM, slot).start()

    def start_next_load(u, slot):
        left = n - 2 * (u + 1)
        nxt = next_ref[e]
        nxt_c = jnp.maximum(nxt, 0)
        block = jnp.where(left > 0, b0 + 2 * (u + 1), bstart_ref[nxt_c])
        blocks_left = jnp.where(left > 0, left, jnp.where(nxt >= 0, nblk_ref[nxt_c], 0))
        start_load(block, blocks_left, slot)

    def wait_store(slot):
        pending = state_ref[ST_PENDING + slot]
        for rows in (big, MOE_BM):
            @pl.when(pending == rows)
            def _():
                y_copy(0, rows, slot).wait()

        state_ref[ST_PENDING + slot] = 0

    def run_unit(u, rows):
        slot = (state_ref[ST_UNITS] + u) % 2
        x_copy(0, rows, slot).wait()
        start_next_load(u, 1 - slot)
        wait_store(slot)
        compute(slot, rows)
        y_copy(b0 + 2 * u, rows, slot).start()
        state_ref[ST_PENDING + slot] = rows

    def compute(slot, rows):
        x_lo, x_hi = _unpack_halves(xbuf_ref[slot, pl.ds(0, rows), :])
        xb = jnp.concatenate([x_lo.astype(BF16), x_hi.astype(BF16)], axis=1)
        acc = jnp.broadcast_to(bd_ref[...], (rows, D_MODEL))
        for c in range(D_FF // FF_CHUNK):
            lo = c * FF_CHUNK
            glu = (jnp.dot(xb, wub_ref[:, lo:lo + FF_CHUNK], preferred_element_type=F32)
                   + bu_ref[:, lo:lo + FF_CHUNK])
            lin = (jnp.dot(xb, wub_ref[:, D_FF + lo:D_FF + lo + FF_CHUNK], preferred_element_type=F32)
                   + bu_ref[:, D_FF + lo:D_FF + lo + FF_CHUNK])
            glu = jnp.minimum(glu, SWIGLU_LIMIT)
            lin = jnp.clip(lin, -SWIGLU_LIMIT, SWIGLU_LIMIT)
            act = glu * jax.nn.sigmoid(SWIGLU_ALPHA * glu) * (lin + 1.0)
            acc = acc + jnp.dot(act.astype(BF16), wdb_ref[lo:lo + FF_CHUNK, :], preferred_element_type=F32)
        ybuf_ref[slot, pl.ds(0, rows), :] = _pack_halves(acc.astype(BF16).astype(F32))

    @pl.when(n > 0)
    def _():
        @pl.when(state_ref[ST_UNITS] == 0)
        def _():
            start_load(b0, n, 0)

        def cast_rows(i, carry):
            r = pl.multiple_of(i * CAST_ROWS, CAST_ROWS)
            wub_ref[pl.ds(r, CAST_ROWS), :] = wu_ref[pl.ds(r, CAST_ROWS), :].astype(BF16)
            wdb_ref[pl.ds(r, CAST_ROWS), :] = wd_ref[pl.ds(r, CAST_ROWS), :].astype(BF16)
            return carry

        lax.fori_loop(0, D_MODEL // CAST_ROWS, cast_rows, 0)

        def full_unit(u, carry):
            run_unit(u, big)
            return carry

        lax.fori_loop(0, n_full, full_unit, 0)

        @pl.when(odd == 1)
        def _():
            run_unit(n_full, MOE_BM)

        state_ref[ST_UNITS] = state_ref[ST_UNITS] + n_full + odd

    @pl.when(e == N_EXPERTS - 1)
    def _():
        wait_store(0)
        wait_store(1)


def _experts(nblk, bstart, rows, w_up, b_up, w_down, b_down):
    n_pad = rows.shape[0]
    per_expert = lambda *tail: pl.BlockSpec((None, *tail), lambda e, nb, bs, nx: (e, 0, 0))
    ids = jnp.arange(N_EXPERTS, dtype=jnp.int32)
    later = (ids[None, :] > ids[:, None]) & (nblk[None, :] > 0)
    nxt = jnp.where(later.any(axis=1), jnp.argmax(later, axis=1), -1).astype(jnp.int32)
    return pl.pallas_call(
        _experts_kernel,
        out_shape=jax.ShapeDtypeStruct((n_pad, ROW_WORDS), jnp.uint32),
        grid_spec=pltpu.PrefetchScalarGridSpec(
            num_scalar_prefetch=3, grid=(N_EXPERTS,),
            in_specs=[pl.BlockSpec(memory_space=pl.ANY),
                      per_expert(D_MODEL, 2 * D_FF), per_expert(1, 2 * D_FF),
                      per_expert(D_FF, D_MODEL), per_expert(1, D_MODEL)],
            out_specs=pl.BlockSpec(memory_space=pl.ANY),
            scratch_shapes=[pltpu.VMEM((D_MODEL, 2 * D_FF), BF16), pltpu.VMEM((D_FF, D_MODEL), BF16),
                            pltpu.VMEM((2, 2 * MOE_BM, ROW_WORDS), jnp.uint32),
                            pltpu.VMEM((2, 2 * MOE_BM, ROW_WORDS), jnp.uint32),
                            pltpu.SMEM((3,), jnp.int32),
                            pltpu.SemaphoreType.DMA((2,)), pltpu.SemaphoreType.DMA((2,))]),
        compiler_params=_cparams("arbitrary"),
        name="experts",
    )(nblk, bstart, nxt, rows, w_up, b_up, w_down, b_down)


COMBINE_TM = 512


def _gather_rows(slots, y_rows, tokens):
    n_chunks = _sc_worker_chunks(tokens)

    @functools.partial(
        pl.kernel, mesh=_sc_mesh(),
        out_type=jax.ShapeDtypeStruct((TOP_K, tokens, ROW_WORDS), jnp.uint32),
        scratch_types=[pltpu.VMEM((TOP_K, SC_CHUNK), jnp.int32),
                       pltpu.VMEM((SC_CHUNK, ROW_WORDS), jnp.uint32),
                       pltpu.SemaphoreType.DMA],
        name="sc_gather")
    def gather(y_hbm, slot_hbm, out_hbm, idx_v, rows_v, sem):
        worker = lax.axis_index("s") * SC_CORES + lax.axis_index("c")

        @pl.loop(0, n_chunks)
        def _(ci):
            chunk = worker * n_chunks + ci
            pltpu.sync_copy(slot_hbm.at[chunk], idx_v)
            for k in range(TOP_K):
                pltpu.async_copy(y_hbm.at[idx_v.at[k]], rows_v, sem).wait()
                pltpu.sync_copy(rows_v, out_hbm.at[k, pl.ds(chunk * SC_CHUNK, SC_CHUNK)])

    return gather(y_rows, slots)


def _combine_kernel(xn_ref, gt_ref, mod_ref, fg_ref, rows_ref, out_ref):
    gt = gt_ref[...]
    y_lo = y_hi = None
    for k in range(TOP_K):
        lo, hi = _unpack_halves(rows_ref[k])
        g = gt[:, k:k + 1]
        y_lo = g * lo if y_lo is None else y_lo + g * lo
        y_hi = g * hi if y_hi is None else y_hi + g * hi
    y = jnp.concatenate([y_lo, y_hi], axis=1)
    xo = xn_ref[...] + mod_ref[5:6, :] * y
    out_ref[...] = _rmsnorm_rows(xo, fg_ref[...])


def _combine(xn, gt, mod8, mod_off, final_g, gathered, batch, seq):
    tm = COMBINE_TM
    tokens = batch * seq
    tiles_per_row = seq // tm
    return pl.pallas_call(
        _combine_kernel,
        out_shape=jax.ShapeDtypeStruct((tokens, D_MODEL), F32),
        grid=(tokens // tm,),
        in_specs=[pl.BlockSpec((tm, D_MODEL), lambda i: (i, 0)),
                  pl.BlockSpec((tm, LANES), lambda i: (i, 0)),
                  pl.BlockSpec((None, 8, D_MODEL), lambda i: (mod_off + i // tiles_per_row, 0, 0)),
                  pl.BlockSpec((1, D_MODEL), lambda i: (0, 0)),
                  pl.BlockSpec((TOP_K, tm, ROW_WORDS), lambda i: (0, i, 0))],
        out_specs=pl.BlockSpec((tm, D_MODEL), lambda i: (i, 0)),
        compiler_params=_cparams("arbitrary"),
        name="combine",
    )(xn, gt, mod8, final_g, gathered)


def _block_plan(counts):
    nblk = (counts + MOE_BM - 1) // MOE_BM
    bstart = jnp.cumsum(nblk) - nblk
    return (bstart * MOE_BM).astype(jnp.int32), nblk.astype(jnp.int32), bstart.astype(jnp.int32)


def _run_group(x, mod8, mod_off, p):
    batch, seq, _ = x.shape
    tokens = batch * seq
    x2 = x.reshape(tokens, D_MODEL)
    qkv, z = _in_proj(x2, mod8, mod_off, p["g1"], p["w_in"], _rope_tables(seq), batch, seq, tm=512)
    attn_outs = [_attention_pattern(*qkv_d, batch, seq, d) for qkv_d, d in zip(qkv, DILATIONS)]
    cnt0 = jnp.zeros((N_EXPERTS, LANES), F32)
    xn, h2, e8, r8, gt, cnt = _post(x2, mod8, mod_off, attn_outs, z, p["conv_w"], p["conv_b"], p["ln_g"],
                                    p["ln_b"], p["w_out"], p["g2"], p["w_router2"], p["b_router_t"],
                                    cnt0, batch, seq)
    n_pad = tokens * TOP_K + N_EXPERTS * MOE_BM
    pstart, nblk, bstart = _block_plan(cnt[:, 0].astype(jnp.int32))
    slots = _slots(pstart, e8, r8)
    rows = _dispatch(slots, h2, n_pad)
    y_rows = _experts(nblk, bstart, rows, p["w_up"], p["b_up"], p["w_down"], p["b_down"])
    gathered = _gather_rows(slots, y_rows, tokens)
    out = _combine(xn, gt, mod8, mod_off, p["final_g"], gathered, batch, seq)
    return out.reshape(batch, seq, D_MODEL)


def _prepare(w_ada, b_ada, norm_mix_g, w_in, conv_w, conv_b, conv_ln_g, conv_ln_b, w_out, norm_ffn_g,
             w_router, b_router, w_up, b_up, w_down, b_down, final_g):
    wr = w_router[0]
    wr_hi = wr.astype(BF16)
    wr_lo = (wr - wr_hi.astype(F32)).astype(BF16)
    w_router2 = jnp.concatenate(
        [wr_hi, wr_lo, jnp.zeros((D_MODEL, ROUTER_W - 2 * N_EXPERTS), BF16)], axis=1)
    return dict(
        g1=norm_mix_g[0][None, :], w_in=w_in[0].astype(BF16),
        conv_w=jnp.concatenate([conv_w[0], jnp.zeros((32 - CONV_K, CONV_W), F32)], axis=0),
        conv_b=conv_b[0][None, :], ln_g=conv_ln_g[0][None, :], ln_b=conv_ln_b[0][None, :],
        w_out=w_out[0].astype(BF16), g2=norm_ffn_g[0][None, :],
        w_router2=w_router2,
        b_router_t=jnp.broadcast_to(b_router[0][:, None], (N_EXPERTS, POST_TM)),
        w_up=w_up[0], b_up=b_up[0][:, None, :],
        w_down=w_down[0], b_down=b_down[0][:, None, :],
        final_g=final_g[None, :])


def kernel(x_prompt, x_sample, c_prompt, c_sample, w_ada, b_ada, norm_mix_g, w_in, conv_w, conv_b,
           conv_ln_g, conv_ln_b, w_out, norm_ffn_g, w_router, b_router, w_up, b_up, w_down, b_down,
           final_g):
    p = _prepare(w_ada, b_ada, norm_mix_g, w_in, conv_w, conv_b, conv_ln_g, conv_ln_b, w_out,
                 norm_ffn_g, w_router, b_router, w_up, b_up, w_down, b_down, final_g)
    nb_p, nb_s = c_prompt.shape[0], c_sample.shape[0]
    c_rows = 16
    c_all = jnp.concatenate([c_prompt, c_sample, jnp.zeros((c_rows - nb_p - nb_s, D_MODEL), F32)], axis=0)
    mod = _modulation(c_all, w_ada[0].astype(BF16), b_ada[0][None, :])
    mod8 = jnp.concatenate([mod.reshape(c_rows, 6, D_MODEL), jnp.zeros((c_rows, 2, D_MODEL), F32)], axis=1)
    y_prompt = _run_group(x_prompt, mod8, 0, p)
    y_sample = _run_group(x_sample, mod8, nb_p, p)
    return (y_prompt, y_sample)
```
